```python
import jax, jax.numpy as jnp
from jax import lax
import numpy as np

D_MODEL = 1024
BATCH = 4
SEQ = 4096
DEPTH = 2
DEC_BATCH = 32
DEC_SEQ = 1
PAST_LEN = 8192
PAGE_SIZE = 128

N_MIXERS = 2
N_ATTN_LAYERS = (DEPTH + 1) // 2
N_CONV_LAYERS = DEPTH // 2
N_HEADS = 16
HEAD_DIM = D_MODEL // N_HEADS
CONV_WIDTH = 3
D_FF = -(-8 * D_MODEL // (3 * 256)) * 256
Q_BLOCK = 128
RMS_EPS = 1e-6
FORGET_BIAS_INIT = 3.0
NEG_INF = -1e30

kernel_name = "fox_shortconv_hybrid_step"


def rmsnorm(x, g):
    xf = x.astype(jnp.float32)
    y = xf * lax.rsqrt(jnp.mean(xf * xf, axis=-1, keepdims=True) + RMS_EPS)
    return (y * g.astype(jnp.float32)).astype(x.dtype)


def swiglu(x, w_in, w_out):
    g, u = jnp.split(x @ w_in, 2, axis=-1)
    return (jax.nn.silu(g) * u) @ w_out


def fox_project(h, w_qkvf, b_f):
    B, T, _ = h.shape
    proj = h @ w_qkvf
    q, k, v, f = jnp.split(proj, [D_MODEL, 2 * D_MODEL, 3 * D_MODEL], axis=-1)
    q = q.reshape(B, T, N_HEADS, HEAD_DIM)
    k = k.reshape(B, T, N_HEADS, HEAD_DIM)
    v = v.reshape(B, T, N_HEADS, HEAD_DIM)
    logf = jax.nn.log_sigmoid((f + b_f).astype(jnp.float32))
    return q, k, v, logf


def fox_attend(q, cum_q, q_pos, k, v, cum_kT, k_pos):
    s = jnp.einsum('bqhd,bkhd->bhqk', q, k).astype(jnp.float32) * (HEAD_DIM ** -0.5)
    s = s + jnp.transpose(cum_q, (0, 2, 1))[..., None] - cum_kT[:, :, None, :]
    mask = k_pos[None, :] <= q_pos[:, None]
    s = jnp.where(mask[None, None], s, NEG_INF)
    p = jax.nn.softmax(s, axis=-1)
    return jnp.einsum('bhqk,bkhd->bqhd', p.astype(v.dtype), v)


def fox_prompt(q, k, v, logf):
    B, S, H, Dh = q.shape
    cum = jnp.cumsum(logf, axis=1)
    cum_kT = jnp.transpose(cum, (0, 2, 1))
    nb = S // Q_BLOCK
    qb = jnp.transpose(q.reshape(B, nb, Q_BLOCK, H, Dh), (1, 0, 2, 3, 4))
    cb = jnp.transpose(cum.reshape(B, nb, Q_BLOCK, H), (1, 0, 2, 3))
    starts = jnp.arange(nb, dtype=jnp.int32) * Q_BLOCK
    k_pos = jnp.arange(S, dtype=jnp.int32)

    def block(args):
        q_i, c_i, s0 = args
        q_pos = s0 + jnp.arange(Q_BLOCK, dtype=jnp.int32)
        return fox_attend(q_i, c_i, q_pos, k, v, cum_kT, k_pos)

    o = lax.map(block, (qb, cb, starts))
    return jnp.transpose(o, (1, 0, 2, 3, 4)).reshape(B, S, H * Dh)


def fox_sample(q, k, v, logf, ck, cv, clogf, page_table):
    DB, T, H, Dh = q.shape
    n_pages = page_table.shape[1]
    past = n_pages * PAGE_SIZE
    k_past = ck[page_table].reshape(DB, past, H, Dh)
    v_past = cv[page_table].reshape(DB, past, H, Dh)
    lf_past = clogf[page_table].reshape(DB, past, H).astype(jnp.float32)
    cum_past = jnp.cumsum(lf_past, axis=1)
    cum_new = cum_past[:, -1:] + jnp.cumsum(logf, axis=1)
    k_all = jnp.concatenate([k_past, k.astype(k_past.dtype)], axis=1)
    v_all = jnp.concatenate([v_past, v.astype(v_past.dtype)], axis=1)
    cum_kT = jnp.transpose(jnp.concatenate([cum_past, cum_new], axis=1), (0, 2, 1))
    q_pos = past + jnp.arange(T, dtype=jnp.int32)
    k_pos = jnp.arange(past + T, dtype=jnp.int32)
    o = fox_attend(q, cum_new, q_pos, k_all, v_all, cum_kT, k_pos)
    return o.reshape(DB, T, H * Dh)


def short_conv_mixer(h, prev, w_in, conv_k, w_out):
    T = h.shape[1]
    b, c, u = jnp.split(h @ w_in, 3, axis=-1)
    cu = c * u
    buf = jnp.concatenate([prev.astype(cu.dtype), cu], axis=1)
    y = conv_k[0] * buf[:, 0:T]
    for j in range(1, CONV_WIDTH):
        y = y + conv_k[j] * buf[:, j:j + T]
    return (b * y) @ w_out, buf[:, -(CONV_WIDTH - 1):]


def setup_inputs(seed: int = 0) -> dict:
    key = jax.random.key(seed)
    ks = jax.random.split(key, 24)
    n_pages = PAST_LEN // PAGE_SIZE
    n_used = DEC_BATCH * n_pages
    n_pool = n_used + (n_used + 3) // 4
    f32 = jnp.float32
    nrm = lambda k, shape, scale: jax.random.normal(k, shape, f32) * scale
    page_table = jax.random.permutation(ks[6], n_pool)[:n_used].reshape(DEC_BATCH, n_pages).astype(jnp.int32)
    return {
        "x_prompt": nrm(ks[0], (BATCH, SEQ, D_MODEL), 1.0),
        "x_sample": nrm(ks[1], (DEC_BATCH, DEC_SEQ, D_MODEL), 1.0),
        "cache_k": nrm(ks[2], (N_ATTN_LAYERS, n_pool, PAGE_SIZE, N_HEADS, HEAD_DIM), 1.0),
        "cache_v": nrm(ks[3], (N_ATTN_LAYERS, n_pool, PAGE_SIZE, N_HEADS, HEAD_DIM), 1.0),
        "cache_logf": jax.nn.log_sigmoid(FORGET_BIAS_INIT + nrm(ks[4], (N_ATTN_LAYERS, n_pool, PAGE_SIZE, N_HEADS), 1.0)),
        "state_conv": nrm(ks[5], (N_CONV_LAYERS, DEC_BATCH, CONV_WIDTH - 1, D_MODEL), 1.0),
        "page_table": page_table,
        "norm_mix": 1.0 + nrm(ks[7], (DEPTH, D_MODEL), 0.02),
        "norm_ffn": 1.0 + nrm(ks[8], (DEPTH, D_MODEL), 0.02),
        "w_ffn_in": nrm(ks[9], (DEPTH, D_MODEL, 2 * D_FF), D_MODEL ** -0.5),
        "w_ffn_out": nrm(ks[10], (DEPTH, D_FF, D_MODEL), D_FF ** -0.5),
        "norm_final": 1.0 + nrm(ks[11], (D_MODEL,), 0.02),
        "attn_w_qkvf": nrm(ks[12], (N_ATTN_LAYERS, D_MODEL, 3 * D_MODEL + N_HEADS), D_MODEL ** -0.5),
        "attn_b_f": FORGET_BIAS_INIT + nrm(ks[13], (N_ATTN_LAYERS, N_HEADS), 0.1),
        "attn_w_o": nrm(ks[14], (N_ATTN_LAYERS, D_MODEL, D_MODEL), D_MODEL ** -0.5),
        "conv_w_in": nrm(ks[15], (N_CONV_LAYERS, D_MODEL, 3 * D_MODEL), D_MODEL ** -0.5),
        "conv_kernel": nrm(ks[16], (N_CONV_LAYERS, CONV_WIDTH, D_MODEL), CONV_WIDTH ** -0.5),
        "conv_w_out": nrm(ks[17], (N_CONV_LAYERS, D_MODEL, D_MODEL), D_MODEL ** -0.5),
    }


def reference(x_prompt, x_sample, cache_k, cache_v, cache_logf, state_conv, page_table,
              norm_mix, norm_ffn, w_ffn_in, w_ffn_out, norm_final,
              attn_w_qkvf, attn_b_f, attn_w_o, conv_w_in, conv_kernel, conv_w_out):
    yp, ys = x_prompt, x_sample
    kp_l, vp_l, lp_l, ks_l, vs_l, ls_l, cp_l, cs_l = [], [], [], [], [], [], [], []
    for i in range(DEPTH):
        hp = rmsnorm(yp, norm_mix[i])
        hs = rmsnorm(ys, norm_mix[i])
        li = i // N_MIXERS
        if i % N_MIXERS == 0:
            qp, kp, vp, lfp = fox_project(hp, attn_w_qkvf[li], attn_b_f[li])
            qs, ks_, vs_, lfs = fox_project(hs, attn_w_qkvf[li], attn_b_f[li])
            mp = fox_prompt(qp, kp, vp, lfp) @ attn_w_o[li]
            ms = fox_sample(qs, ks_, vs_, lfs, cache_k[li], cache_v[li], cache_logf[li], page_table) @ attn_w_o[li]
            kp_l.append(kp); vp_l.append(vp); lp_l.append(lfp)
            ks_l.append(ks_); vs_l.append(vs_); ls_l.append(lfs)
        else:
            zeros_prev = jnp.zeros((yp.shape[0], CONV_WIDTH - 1, D_MODEL), yp.dtype)
            mp, cpn = short_conv_mixer(hp, zeros_prev, conv_w_in[li], conv_kernel[li], conv_w_out[li])
            ms, csn = short_conv_mixer(hs, state_conv[li], conv_w_in[li], conv_kernel[li], conv_w_out[li])
            cp_l.append(cpn); cs_l.append(csn)
        yp = yp + mp
        ys = ys + ms
        yp = yp + swiglu(rmsnorm(yp, norm_ffn[i]), w_ffn_in[i], w_ffn_out[i])
        ys = ys + swiglu(rmsnorm(ys, norm_ffn[i]), w_ffn_in[i], w_ffn_out[i])
    y_prompt = rmsnorm(yp, norm_final)
    y_sample = rmsnorm(ys, norm_final)
    return (y_prompt, y_sample,
            jnp.stack(kp_l), jnp.stack(vp_l), jnp.stack(lp_l),
            jnp.stack(ks_l), jnp.stack(vs_l), jnp.stack(ls_l),
            jnp.stack(cp_l), jnp.stack(cs_l))
```

```python
import functools

import jax
import jax.numpy as jnp
from jax import lax
from jax.experimental import pallas as pl
from jax.experimental.pallas import tpu as pltpu

F32 = jnp.float32
BF16 = jnp.bfloat16

RMS_EPS = 1e-6
NEG_INF = -1e30
N_HEADS = 16
HEAD_DIM = 64
PAGE_SIZE = 128
LANES = 128
VMEM_LIMIT = 56 * 1024 * 1024

PROJ_TM = 512
ATTN_T = 512
FFN_TM = 1024
FFN_TC = 256
CONV_TM = 512
DEC_G = 4


def _cparams(sem):
    return pltpu.CompilerParams(dimension_semantics=sem, vmem_limit_bytes=VMEM_LIMIT)


def _rms(x, g):
    ms = jnp.mean(x * x, axis=-1, keepdims=True)
    return x * lax.rsqrt(ms + RMS_EPS) * g


def _dot(a, b):
    return jnp.dot(a, b, preferred_element_type=F32)


def _nt_dot(a, b):
    return lax.dot_general(a, b, (((1,), (1,)), ((), ())), preferred_element_type=F32)


def _log_sigmoid(x):
    return jnp.minimum(x, 0.0) - jnp.log1p(jnp.exp(-jnp.abs(x)))


def _split3(x):
    hi = x.astype(BF16)
    r1 = x - hi.astype(F32)
    mid = r1.astype(BF16)
    lo = (r1 - mid.astype(F32)).astype(BF16)
    return hi, mid, lo


def _dot3(x, w):
    hi, mid, lo = _split3(x)
    return _dot(hi, w) + _dot(mid, w) + _dot(lo, w)


def _nt_dot3(w, x):
    hi, mid, lo = _split3(x)
    return _nt_dot(w, hi) + _nt_dot(w, mid) + _nt_dot(w, lo)


def _proj_prompt_kernel(x_ref, g_ref, wq_ref, wkT_ref, wv_ref, wvT_ref, wfT_ref, bf_ref,
                        q_ref, kT_ref, kTb_ref, v_ref, vT_ref, lfT_ref, *, scale):
    h = _rms(x_ref[...], g_ref[...]).astype(BF16)
    q_ref[...] = (_dot(h, wq_ref[...]) * scale).astype(BF16)
    kT = _nt_dot(wkT_ref[...], h)
    kT_ref[...] = kT
    kTb_ref[...] = kT.astype(BF16)
    v_ref[...] = _dot(h, wv_ref[...]).astype(BF16)
    vT_ref[...] = _nt_dot(wvT_ref[...], h)
    lfT_ref[...] = _log_sigmoid(_nt_dot(wfT_ref[...], h) + bf_ref[...])


def _proj_prompt(x, g, wq, wkT, wv, wvT, wfT, bf, *, batch, seq):
    m, d = x.shape
    tm = PROJ_TM
    nb = seq // tm
    const = lambda i: (0, 0)
    tmaj = lambda i: (i // nb, 0, i % nb)
    return pl.pallas_call(
        functools.partial(_proj_prompt_kernel, scale=HEAD_DIM ** -0.5),
        grid=(m // tm,),
        in_specs=[
            pl.BlockSpec((tm, d), lambda i: (i, 0)),
            pl.BlockSpec((1, d), const),
            pl.BlockSpec((d, d), const),
            pl.BlockSpec((d, d), const),
            pl.BlockSpec((d, d), const),
            pl.BlockSpec((d, d), const),
            pl.BlockSpec((N_HEADS, d), const),
            pl.BlockSpec((N_HEADS, 1), const),
        ],
        out_specs=[
            pl.BlockSpec((tm, d), lambda i: (i, 0)),
            pl.BlockSpec((None, d, tm), tmaj),
            pl.BlockSpec((None, d, tm), tmaj),
            pl.BlockSpec((tm, d), lambda i: (i, 0)),
            pl.BlockSpec((None, d, tm), tmaj),
            pl.BlockSpec((None, N_HEADS, tm), tmaj),
        ],
        out_shape=[
            jax.ShapeDtypeStruct((m, d), BF16),
            jax.ShapeDtypeStruct((batch, d, seq), F32),
            jax.ShapeDtypeStruct((batch, d, seq), BF16),
            jax.ShapeDtypeStruct((m, d), BF16),
            jax.ShapeDtypeStruct((batch, d, seq), F32),
            jax.ShapeDtypeStruct((batch, N_HEADS, seq), F32),
        ],
        compiler_params=_cparams(("arbitrary",)),
        name="proj_prompt",
    )(x, g, wq, wkT, wv, wvT, wfT, bf)


def _proj_sample_kernel(x_ref, g_ref, wqT_ref, wkT_ref, wvT_ref, wfT_ref, bf_ref,
                        qT_ref, kT_ref, vT_ref, lfT_ref, *, scale):
    h = _rms(x_ref[...], g_ref[...]).astype(BF16)
    qT_ref[...] = _nt_dot(wqT_ref[...], h) * scale
    kT_ref[...] = _nt_dot(wkT_ref[...], h)
    vT_ref[...] = _nt_dot(wvT_ref[...], h)
    lfT_ref[...] = _log_sigmoid(_nt_dot(wfT_ref[...], h) + bf_ref[...])


def _proj_sample(x, g, wqT, wkT, wvT, wfT, bf):
    n, d = x.shape
    return pl.pallas_call(
        functools.partial(_proj_sample_kernel, scale=HEAD_DIM ** -0.5),
        out_shape=[
            jax.ShapeDtypeStruct((d, n), F32),
            jax.ShapeDtypeStruct((d, n), F32),
            jax.ShapeDtypeStruct((d, n), F32),
            jax.ShapeDtypeStruct((N_HEADS, n), F32),
        ],
        compiler_params=pltpu.CompilerParams(vmem_limit_bytes=VMEM_LIMIT),
        name="proj_sample",
    )(x, g, wqT, wkT, wvT, wfT, bf)


def _cumsum_kernel(lfT_ref, cumT_ref, cum_ref, *, seq):
    blk = LANES
    r = lax.broadcasted_iota(jnp.int32, (blk, blk), 0)
    c = lax.broadcasted_iota(jnp.int32, (blk, blk), 1)
    upper = (r <= c).astype(BF16)
    eye = (r == c).astype(BF16)
    carry = jnp.zeros((N_HEADS, 1), F32)
    for i in range(seq // blk):
        sl = slice(i * blk, (i + 1) * blk)
        cb = _dot3(lfT_ref[:, sl], upper) + carry
        cumT_ref[:, sl] = cb
        cum_ref[sl, :] = _nt_dot3(eye, cb)
        carry = cb[:, blk - 1:blk]


def _cumsum(lfT):
    batch, nh, seq = lfT.shape
    return pl.pallas_call(
        functools.partial(_cumsum_kernel, seq=seq),
        grid=(batch,),
        in_specs=[pl.BlockSpec((None, nh, seq), lambda b: (b, 0, 0))],
        out_specs=[
            pl.BlockSpec((None, nh, seq), lambda b: (b, 0, 0)),
            pl.BlockSpec((None, seq, nh), lambda b: (b, 0, 0)),
        ],
        out_shape=[
            jax.ShapeDtypeStruct((batch, nh, seq), F32),
            jax.ShapeDtypeStruct((batch, seq, nh), F32),
        ],
        compiler_params=_cparams(("arbitrary",)),
        name="cumsum",
    )(lfT)


def _attn_kernel(q_ref, kT_ref, v_ref, cum_ref, cumT_ref, o_ref, m_ref, l_ref, acc_ref, *, t):
    i = pl.program_id(1)
    j = pl.program_id(2)

    @pl.when(j == 0)
    def _():
        m_ref[...] = jnp.full(m_ref.shape, NEG_INF, F32)
        l_ref[...] = jnp.zeros(l_ref.shape, F32)
        acc_ref[...] = jnp.zeros(acc_ref.shape, F32)

    def step(diagonal):
        lane = lax.broadcasted_iota(jnp.int32, (1, LANES), 1)
        first_half = lane < HEAD_DIM
        if diagonal:
            keep = (lax.broadcasted_iota(jnp.int32, (t, t), 1)
                    <= lax.broadcasted_iota(jnp.int32, (t, t), 0))
        for hp in range(N_HEADS // 2):
            sl = slice(hp * LANES, (hp + 1) * LANES)
            q2 = q_ref[:, sl]
            kT2 = kT_ref[sl, :]
            v2 = v_ref[:, sl]
            m2 = m_ref[:, sl]
            l2 = l_ref[:, sl]
            pv = None
            stats = []
            for e in range(2):
                h = 2 * hp + e
                half = first_half if e == 0 else jnp.logical_not(first_half)
                qm = jnp.where(half, q2, jnp.zeros_like(q2))
                s = _dot(qm, kT2)
                s = s + (cum_ref[:, h:h + 1] - cumT_ref[h:h + 1, :])
                if diagonal:
                    s = jnp.where(keep, s, NEG_INF)
                m_prev = m2[:, e * HEAD_DIM:e * HEAD_DIM + 1]
                l_prev = l2[:, e * HEAD_DIM:e * HEAD_DIM + 1]
                m_next = jnp.maximum(m_prev, jnp.max(s, axis=1, keepdims=True))
                p = jnp.exp(s - m_next)
                alpha = jnp.exp(m_prev - m_next)
                l_next = alpha * l_prev + jnp.sum(p, axis=1, keepdims=True)
                vm = jnp.where(half, v2, jnp.zeros_like(v2))
                c = _dot(p.astype(BF16), vm)
                pv = c if pv is None else pv + c
                stats.append((m_next, l_next, alpha))
            (m0, l0, a0), (m1, l1, a1) = stats
            m_ref[:, sl] = jnp.where(first_half, m0, m1)
            l_ref[:, sl] = jnp.where(first_half, l0, l1)
            acc_ref[:, sl] = jnp.where(first_half, a0, a1) * acc_ref[:, sl] + pv

    @pl.when(j < i)
    def _():
        step(False)

    @pl.when(j == i)
    def _():
        step(True)
        o_ref[...] = (acc_ref[...] / l_ref[...]).astype(BF16)


def _attention(q, kTb, v, cum, cumT, *, batch, seq):
    m, d = q.shape
    t = ATTN_T
    nq = seq // t
    qmap = lambda b, i, j: (b * nq + i, 0)
    kvrow = lambda b, i, j: (b * nq + jnp.minimum(j, i), 0)
    kvcol = lambda b, i, j: (b, 0, jnp.minimum(j, i))
    return pl.pallas_call(
        functools.partial(_attn_kernel, t=t),
        grid=(batch, nq, nq),
        in_specs=[
            pl.BlockSpec((t, d), qmap),
            pl.BlockSpec((None, d, t), kvcol),
            pl.BlockSpec((t, d), kvrow),
            pl.BlockSpec((t, N_HEADS), qmap),
            pl.BlockSpec((None, N_HEADS, t), kvcol),
        ],
        out_specs=pl.BlockSpec((t, d), qmap),
        out_shape=jax.ShapeDtypeStruct((m, d), BF16),
        scratch_shapes=[pltpu.VMEM((t, d), F32), pltpu.VMEM((t, d), F32), pltpu.VMEM((t, d), F32)],
        compiler_params=_cparams(("arbitrary", "arbitrary", "arbitrary")),
        name="attention",
    )(q, kTb, v, cum, cumT)


def _decode_kernel(pt_ref, qT_ref, kTs_ref, vTs_ref, lfTs_ref, *rest, n_req, group, n_steps):
    k_refs = rest[:group]
    v_refs = rest[group:2 * group]
    lf_refs = rest[2 * group:3 * group]
    o_ref = rest[3 * group]
    qb_ref, acc_ref, m_ref, l_ref, carry_ref, s_ref, p_ref, ab_ref = rest[3 * group + 1:]
    b = pl.program_id(0)
    j = pl.program_id(1)
    d_model = N_HEADS * HEAD_DIM

    @pl.when(j == 0)
    def _():
        sel = lax.broadcasted_iota(jnp.int32, (1, n_req), 1) == b
        pick = lambda ref: jnp.sum(jnp.where(sel, ref[...], 0.0), axis=1, keepdims=True)
        qcol = pick(qT_ref)
        kcol = pick(kTs_ref)
        vcol = pick(vTs_ref)
        qk = qcol * kcol
        lane0 = lax.broadcasted_iota(jnp.int32, (1, LANES), 1) == 0
        for h in range(N_HEADS):
            hs = slice(h * HEAD_DIM, (h + 1) * HEAD_DIM)
            qb_ref[h] = jnp.broadcast_to(qcol[hs], (HEAD_DIM, LANES))
            acc_ref[h] = jnp.where(lane0, vcol[hs], 0.0)
            m_ref[h:h + 1, :] = jnp.sum(qk[hs], axis=0, keepdims=True)
        l_ref[...] = jnp.ones(l_ref.shape, F32)
        carry_ref[...] = pick(lfTs_ref)

    r = lax.broadcasted_iota(jnp.int32, (PAGE_SIZE, PAGE_SIZE), 0)
    c = lax.broadcasted_iota(jnp.int32, (PAGE_SIZE, PAGE_SIZE), 1)
    after = (r > c).astype(BF16)

    for g in range(group):
        k_ref, v_ref, lf_ref = k_refs[g], v_refs[g], lf_refs[g]
        for h in range(N_HEADS):
            s_ref[h:h + 1, :] = jnp.sum(k_ref[h] * qb_ref[h], axis=0, keepdims=True)
        lfT = lf_ref[...]
        carry = carry_ref[...]
        logit = s_ref[...] + (_dot3(lfT, after) + carry)
        carry_ref[...] = carry + jnp.sum(lfT, axis=1, keepdims=True)
        m_prev = m_ref[...]
        m_next = jnp.maximum(m_prev, jnp.max(logit, axis=1, keepdims=True))
        p = jnp.exp(logit - m_next)
        alpha = jnp.exp(m_prev - m_next)
        l_ref[...] = alpha * l_ref[...] + jnp.sum(p, axis=1, keepdims=True)
        m_ref[...] = m_next
        p_ref[...] = p
        ab_ref[...] = jnp.broadcast_to(alpha, ab_ref.shape)
        for h in range(N_HEADS):
            acc_ref[h] = ab_ref[h:h + 1, :] * acc_ref[h] + v_ref[h] * p_ref[h:h + 1, :]

    @pl.when(j == n_steps - 1)
    def _():
        for h in range(N_HEADS):
            acc_ref[h] = acc_ref[h] / l_ref[h:h + 1, :]
        ones = jnp.ones((8, LANES), BF16)
        o8 = _nt_dot3(ones, acc_ref[...].reshape(d_model, LANES))
        o_ref[...] = o8[0:1, :].astype(BF16)


def _decode_attention(page_table, qT, kTs, vTs, lfTs, kT_cache, vT_cache, lfT_cache):
    n_req, n_pages = page_table.shape
    d = N_HEADS * HEAD_DIM
    group = DEC_G
    n_steps = n_pages // group

    def page(g):
        return lambda b, j, pt: (pt[b, n_pages - 1 - (j * group + g)], 0, 0, 0)

    def page3(g):
        return lambda b, j, pt: (pt[b, n_pages - 1 - (j * group + g)], 0, 0)

    small = lambda b, j, pt: (0, 0)
    kv_block = (None, N_HEADS, HEAD_DIM, PAGE_SIZE)
    in_specs = [
        pl.BlockSpec((d, n_req), small),
        pl.BlockSpec((d, n_req), small),
        pl.BlockSpec((d, n_req), small),
        pl.BlockSpec((N_HEADS, n_req), small),
    ]
    in_specs += [pl.BlockSpec(kv_block, page(g)) for g in range(group)]
    in_specs += [pl.BlockSpec(kv_block, page(g)) for g in range(group)]
    in_specs += [pl.BlockSpec((None, N_HEADS, PAGE_SIZE), page3(g)) for g in range(group)]
    grid_spec = pltpu.PrefetchScalarGridSpec(
        num_scalar_prefetch=1,
        grid=(n_req, n_steps),
        in_specs=in_specs,
        out_specs=pl.BlockSpec((None, 1, d), lambda b, j, pt: (b, 0, 0)),
        scratch_shapes=[
            pltpu.VMEM((N_HEADS, HEAD_DIM, LANES), F32),
            pltpu.VMEM((N_HEADS, HEAD_DIM, LANES), F32),
            pltpu.VMEM((N_HEADS, 1), F32),
            pltpu.VMEM((N_HEADS, 1), F32),
            pltpu.VMEM((N_HEADS, 1), F32),
            pltpu.VMEM((N_HEADS, PAGE_SIZE), F32),
            pltpu.VMEM((N_HEADS, PAGE_SIZE), F32),
            pltpu.VMEM((N_HEADS, LANES), F32),
        ],
    )
    out = pl.pallas_call(
        functools.partial(_decode_kernel, n_req=n_req, group=group, n_steps=n_steps),
        grid_spec=grid_spec,
        out_shape=jax.ShapeDtypeStruct((n_req, 1, d), BF16),
        compiler_params=_cparams(("arbitrary", "arbitrary")),
        name="decode_attention",
    )(page_table, qT, kTs, vTs, lfTs,
      *([kT_cache] * group), *([vT_cache] * group), *([lfT_cache] * group))
    return out.reshape(n_req, d)


def _ffn_kernel(x_ref, z_ref, wmo_ref, g_ref, wg_ref, wu_ref, wo_ref, gf_ref, out_ref,
                acc_ref, h_ref, *, n_chunks, final_norm):
    c = pl.program_id(1)

    @pl.when(c == 0)
    def _():
        y1 = x_ref[...] + _dot(z_ref[...], wmo_ref[...])
        acc_ref[...] = y1
        h_ref[...] = _rms(y1, g_ref[...]).astype(BF16)

    h = h_ref[...]
    gate = _dot(h, wg_ref[...])
    up = _dot(h, wu_ref[...])
    a = (gate * jax.nn.sigmoid(gate) * up).astype(BF16)
    acc_ref[...] += _dot(a, wo_ref[...])

    @pl.when(c == n_chunks - 1)
    def _():
        y = acc_ref[...]
        out_ref[...] = _rms(y, gf_ref[...]) if final_norm else y


def _mix_ffn(x, z, wmo, g, w_in, w_out, g_final, *, tm, final_norm):
    m, d = x.shape
    d_ff = w_out.shape[0]
    tc = FFN_TC
    n_chunks = d_ff // tc
    row = lambda i, c: (i, 0)
    const = lambda i, c: (0, 0)
    return pl.pallas_call(
        functools.partial(_ffn_kernel, n_chunks=n_chunks, final_norm=final_norm),
        grid=(m // tm, n_chunks),
        in_specs=[
            pl.BlockSpec((tm, d), row),
            pl.BlockSpec((tm, d), row),
            pl.BlockSpec((d, d), const),
            pl.BlockSpec((1, d), const),
            pl.BlockSpec((d, tc), lambda i, c: (0, c)),
            pl.BlockSpec((d, tc), lambda i, c: (0, n_chunks + c)),
            pl.BlockSpec((tc, d), lambda i, c: (c, 0)),
            pl.BlockSpec((1, d), const),
        ],
        out_specs=pl.BlockSpec((tm, d), row),
        out_shape=jax.ShapeDtypeStruct((m, d), F32),
        scratch_shapes=[pltpu.VMEM((tm, d), F32), pltpu.VMEM((tm, d), BF16)],
        compiler_params=_cparams(("arbitrary", "arbitrary")),
        name="mix_ffn",
    )(x, z, wmo, g, w_in, w_in, w_out, g_final)


def _conv_prompt_kernel(x_ref, g_ref, wb_ref, wc_ref, wu_ref, ck_ref, prev_ref, z_ref, st_ref,
                        buf_ref, *, tm, blocks_per_seq):
    i = pl.program_id(0)
    h = _rms(x_ref[...], g_ref[...]).astype(BF16)
    gate = _dot(h, wb_ref[...])
    cu = _dot(h, wc_ref[...]) * _dot(h, wu_ref[...])

    @pl.when(i % blocks_per_seq == 0)
    def _():
        buf_ref[6:8, :] = prev_ref[...]

    buf_ref[8:8 + tm, :] = cu
    y = ck_ref[0:1, :] * buf_ref[6:6 + tm, :] + ck_ref[1:2, :] * buf_ref[7:7 + tm, :]
    y = y + ck_ref[2:3, :] * cu
    z_ref[...] = (gate * y).astype(BF16)
    tail = cu[tm - 2:tm, :]
    st_ref[...] = tail
    buf_ref[6:8, :] = tail


def _conv_prompt(x, g, wb, wc, wu, ck, prev, *, batch, seq):
    m, d = x.shape
    tm = CONV_TM
    bps = seq // tm
    const = lambda i: (0, 0)
    return pl.pallas_call(
        functools.partial(_conv_prompt_kernel, tm=tm, blocks_per_seq=bps),
        grid=(m // tm,),
        in_specs=[
            pl.BlockSpec((tm, d), lambda i: (i, 0)),
            pl.BlockSpec((1, d), const),
            pl.BlockSpec((d, d), const),
            pl.BlockSpec((d, d), const),
            pl.BlockSpec((d, d), const),
            pl.BlockSpec((3, d), const),
            pl.BlockSpec((None, 2, d), lambda i: (i // bps, 0, 0)),
        ],
        out_specs=[
            pl.BlockSpec((tm, d), lambda i: (i, 0)),
            pl.BlockSpec((None, 2, d), lambda i: (i // bps, 0, 0)),
        ],
        out_shape=[
            jax.ShapeDtypeStruct((m, d), BF16),
            jax.ShapeDtypeStruct((batch, 2, d), F32),
        ],
        scratch_shapes=[pltpu.VMEM((tm + 8, d), F32)],
        compiler_params=_cparams(("arbitrary",)),
        name="conv_prompt",
    )(x, g, wb, wc, wu, ck, prev)


def _conv_sample_kernel(x_ref, g_ref, wb_ref, wc_ref, wu_ref, ck_ref, p0_ref, p1_ref, z_ref, cu_ref):
    h = _rms(x_ref[...], g_ref[...]).astype(BF16)
    gate = _dot(h, wb_ref[...])
    cu = _dot(h, wc_ref[...]) * _dot(h, wu_ref[...])
    y = ck_ref[0:1, :] * p0_ref[...] + ck_ref[1:2, :] * p1_ref[...]
    y = y + ck_ref[2:3, :] * cu
    z_ref[...] = (gate * y).astype(BF16)
    cu_ref[...] = cu


def _conv_sample(x, g, wb, wc, wu, ck, p0, p1):
    n, d = x.shape
    return pl.pallas_call(
        _conv_sample_kernel,
        out_shape=[jax.ShapeDtypeStruct((n, d), BF16), jax.ShapeDtypeStruct((n, d), F32)],
        compiler_params=pltpu.CompilerParams(vmem_limit_bytes=VMEM_LIMIT),
        name="conv_sample",
    )(x, g, wb, wc, wu, ck, p0, p1)


def kernel(x_prompt, x_sample, cache_k, cache_v, cache_logf, state_conv, page_table,
           norm_mix, norm_ffn, w_ffn_in, w_ffn_out, norm_final,
           attn_w_qkvf, attn_b_f, attn_w_o, conv_w_in, conv_kernel, conv_w_out):
    batch, seq, d = x_prompt.shape
    n_req = x_sample.shape[0]
    assert x_sample.shape[1] == 1 and d == N_HEADS * HEAD_DIM
    assert norm_mix.shape[0] == 2 and cache_k.shape[2] == PAGE_SIZE

    xp = x_prompt.reshape(batch * seq, d)
    xs = x_sample.reshape(n_req, d)
    gfin = norm_final.reshape(1, d)

    w = attn_w_qkvf[0]
    wT = w.T
    wq, wv = w[:, :d].astype(BF16), w[:, 2 * d:3 * d].astype(BF16)
    wqT, wkT, wvT = (wT[k * d:(k + 1) * d].astype(BF16) for k in range(3))
    wfT = wT[3 * d:].astype(BF16)
    bf = attn_b_f[0].reshape(N_HEADS, 1)
    g0 = norm_mix[0].reshape(1, d)

    q, kT, kTb, v, vT, lfT = _proj_prompt(xp, g0, wq, wkT, wv, wvT, wfT, bf, batch=batch, seq=seq)
    cumT, cum = _cumsum(lfT)
    o_p = _attention(q, kTb, v, cum.reshape(batch * seq, N_HEADS), cumT, batch=batch, seq=seq)

    qTs, kTs, vTs, lfTs = _proj_sample(xs, g0, wqT, wkT, wvT, wfT, bf)
    o_s = _decode_attention(
        page_table, qTs, kTs, vTs, lfTs,
        jnp.transpose(cache_k[0], (0, 2, 3, 1)),
        jnp.transpose(cache_v[0], (0, 2, 3, 1)),
        jnp.transpose(cache_logf[0], (0, 2, 1)))

    wo = attn_w_o[0].astype(BF16)
    w_in0, w_out0 = w_ffn_in[0].astype(BF16), w_ffn_out[0].astype(BF16)
    gf0 = norm_ffn[0].reshape(1, d)
    yp = _mix_ffn(xp, o_p, wo, gf0, w_in0, w_out0, gfin, tm=FFN_TM, final_norm=False)
    ys = _mix_ffn(xs, o_s, wo, gf0, w_in0, w_out0, gfin, tm=n_req, final_norm=False)

    wci = conv_w_in[0]
    wb, wc, wu = (wci[:, k * d:(k + 1) * d].astype(BF16) for k in range(3))
    g1 = norm_mix[1].reshape(1, d)
    ck = conv_kernel[0]
    zp, conv_p = _conv_prompt(yp, g1, wb, wc, wu, ck, jnp.zeros((batch, 2, d), F32),
                              batch=batch, seq=seq)
    zs, cu_s = _conv_sample(ys, g1, wb, wc, wu, ck, state_conv[0, :, 0], state_conv[0, :, 1])

    wco = conv_w_out[0].astype(BF16)
    w_in1, w_out1 = w_ffn_in[1].astype(BF16), w_ffn_out[1].astype(BF16)
    gf1 = norm_ffn[1].reshape(1, d)
    yp = _mix_ffn(yp, zp, wco, gf1, w_in1, w_out1, gfin, tm=FFN_TM, final_norm=True)
    ys = _mix_ffn(ys, zs, wco, gf1, w_in1, w_out1, gfin, tm=n_req, final_norm=True)

    to_heads = lambda t: jnp.transpose(t.reshape(batch, N_HEADS, HEAD_DIM, seq), (0, 3, 1, 2))[None]
    to_heads_s = lambda t: t.T.reshape(1, n_req, 1, N_HEADS, HEAD_DIM)
    return (
        yp.reshape(batch, seq, d),
        ys.reshape(n_req, 1, d),
        to_heads(kT),
        to_heads(vT),
        jnp.transpose(lfT, (0, 2, 1))[None],
        to_heads_s(kTs),
        to_heads_s(vTs),
        lfTs.T.reshape(1, n_req, 1, N_HEADS),
        conv_p[None],
        jnp.stack([state_conv[0, :, 1], cu_s], axis=1)[None],
    )
```

```python
import functools

import jax
import jax.numpy as jnp
from jax import lax
from jax.experimental import pallas as pl
from jax.experimental.pallas import tpu as pltpu

F32 = jnp.float32
BF16 = jnp.bfloat16

RMS_EPS = 1e-6
NEG_INF = -1e30
N_HEADS = 16
HEAD_DIM = 64
PAGE_SIZE = 128
LANES = 128
VMEM_LIMIT = 56 * 1024 * 1024

PROJ_TM = 512
ATTN_T = 512
ATTN_CH = 64
LOG2E = 1.4426950408889634
FFN_TM = 1024
FFN_TC = 256
CONV_TM = 512
DEC_G = 8


def _cparams(sem):
    return pltpu.CompilerParams(dimension_semantics=sem, vmem_limit_bytes=VMEM_LIMIT)


def _rms(x, g):
    ms = jnp.mean(x * x, axis=-1, keepdims=True)
    return x * lax.rsqrt(ms + RMS_EPS) * g


def _dot(a, b):
    return jnp.dot(a, b, preferred_element_type=F32)


def _nt_dot(a, b):
    return lax.dot_general(a, b, (((1,), (1,)), ((), ())), preferred_element_type=F32)


def _log_sigmoid(x):
    return jnp.minimum(x, 0.0) - jnp.log1p(jnp.exp(-jnp.abs(x)))


def _split3(x):
    hi = x.astype(BF16)
    r1 = x - hi.astype(F32)
    mid = r1.astype(BF16)
    lo = (r1 - mid.astype(F32)).astype(BF16)
    return hi, mid, lo


def _dot3(x, w):
    hi, mid, lo = _split3(x)
    return _dot(hi, w) + _dot(mid, w) + _dot(lo, w)


def _nt_dot3(w, x):
    hi, mid, lo = _split3(x)
    return _nt_dot(w, hi) + _nt_dot(w, mid) + _nt_dot(w, lo)


def _qkvf_specs(d, index):
    return [
        pl.BlockSpec((d, d), index(0)),
        pl.BlockSpec((d, d), index(1)),
        pl.BlockSpec((d, d), index(2)),
        pl.BlockSpec((N_HEADS, d), index(3 * d // N_HEADS)),
    ]


def _proj_prompt_kernel(x_ref, g_ref, wqT_ref, wkT_ref, wvT_ref, wfT_ref, bf_ref,
                        q_ref, kT_ref, kTb_ref, v_ref, vT_ref, lfT_ref, *, scale):
    h = _rms(x_ref[...], g_ref[...]).astype(BF16)
    q = (_nt_dot(h, wqT_ref[...]) * scale).astype(BF16)
    v = _nt_dot(h, wvT_ref[...]).astype(BF16)
    for hp in range(N_HEADS // 2):
        sl = slice(hp * LANES, (hp + 1) * LANES)
        q_ref[hp] = q[:, sl]
        v_ref[hp] = v[:, sl]
    kT = _nt_dot(wkT_ref[...], h)
    kT_ref[...] = kT
    kTb_ref[...] = kT.astype(BF16)
    vT_ref[...] = _nt_dot(wvT_ref[...], h)
    lfT_ref[...] = _log_sigmoid(_nt_dot(wfT_ref[...], h) + bf_ref[...])


def _proj_prompt(x, g, wT, bf, *, batch, seq):
    m, d = x.shape
    tm = PROJ_TM
    nb = seq // tm
    n_pairs = N_HEADS // 2
    const = lambda i: (0, 0)
    tmaj = lambda i: (i // nb, 0, i % nb)
    return pl.pallas_call(
        functools.partial(_proj_prompt_kernel, scale=HEAD_DIM ** -0.5 * LOG2E),
        grid=(m // tm,),
        in_specs=[
            pl.BlockSpec((tm, d), lambda i: (i, 0)),
            pl.BlockSpec((1, d), const),
            *_qkvf_specs(d, lambda k: (lambda i: (k, 0))),
            pl.BlockSpec((N_HEADS, 1), const),
        ],
        out_specs=[
            pl.BlockSpec((n_pairs, tm, LANES), lambda i: (0, i, 0)),
            pl.BlockSpec((None, d, tm), tmaj),
            pl.BlockSpec((None, d, tm), tmaj),
            pl.BlockSpec((n_pairs, tm, LANES), lambda i: (0, i, 0)),
            pl.BlockSpec((None, d, tm), tmaj),
            pl.BlockSpec((None, N_HEADS, tm), tmaj),
        ],
        out_shape=[
            jax.ShapeDtypeStruct((n_pairs, m, LANES), BF16),
            jax.ShapeDtypeStruct((batch, d, seq), F32),
            jax.ShapeDtypeStruct((batch, d, seq), BF16),
            jax.ShapeDtypeStruct((n_pairs, m, LANES), BF16),
            jax.ShapeDtypeStruct((batch, d, seq), F32),
            jax.ShapeDtypeStruct((batch, N_HEADS, seq), F32),
        ],
        compiler_params=_cparams(("arbitrary",)),
        name="proj_prompt",
    )(x, g, wT, wT, wT, wT, bf)


def _proj_sample_kernel(x_ref, g_ref, wqT_ref, wkT_ref, wvT_ref, wfT_ref, bf_ref,
                        qT_ref, kT_ref, vT_ref, lfT_ref, *, scale):
    h = _rms(x_ref[...], g_ref[...]).astype(BF16)
    qT_ref[...] = _nt_dot(wqT_ref[...], h) * scale
    kT_ref[...] = _nt_dot(wkT_ref[...], h)
    vT_ref[...] = _nt_dot(wvT_ref[...], h)
    lfT_ref[...] = _log_sigmoid(_nt_dot(wfT_ref[...], h) + bf_ref[...])


def _proj_sample(x, g, wT, bf):
    n, d = x.shape
    full = lambda shape: pl.BlockSpec(shape, lambda i: (0, 0))
    return pl.pallas_call(
        functools.partial(_proj_sample_kernel, scale=HEAD_DIM ** -0.5),
        grid=(1,),
        in_specs=[
            full((n, d)),
            full((1, d)),
            *_qkvf_specs(d, lambda k: (lambda i: (k, 0))),
            full((N_HEADS, 1)),
        ],
        out_specs=[full((d, n)), full((d, n)), full((d, n)), full((N_HEADS, n))],
        out_shape=[
            jax.ShapeDtypeStruct((d, n), F32),
            jax.ShapeDtypeStruct((d, n), F32),
            jax.ShapeDtypeStruct((d, n), F32),
            jax.ShapeDtypeStruct((N_HEADS, n), F32),
        ],
        compiler_params=_cparams(("arbitrary",)),
        name="proj_sample",
    )(x, g, wT, wT, wT, wT, bf)


def _cumsum_kernel(lfT_ref, cumT_ref, cum_ref, *, seq):
    blk = LANES
    r = lax.broadcasted_iota(jnp.int32, (blk, blk), 0)
    c = lax.broadcasted_iota(jnp.int32, (blk, blk), 1)
    upper = (r <= c).astype(BF16)
    eye = (r == c).astype(BF16)
    carry = jnp.zeros((N_HEADS, 1), F32)
    for i in range(seq // blk):
        sl = slice(i * blk, (i + 1) * blk)
        cb = _dot3(lfT_ref[:, sl], upper) + carry
        cumT_ref[:, sl] = cb
        cum_ref[sl, :] = _nt_dot3(eye, cb)
        carry = cb[:, blk - 1:blk]


def _cumsum(lfT):
    batch, nh, seq = lfT.shape
    return pl.pallas_call(
        functools.partial(_cumsum_kernel, seq=seq),
        grid=(batch,),
        in_specs=[pl.BlockSpec((None, nh, seq), lambda b: (b, 0, 0))],
        out_specs=[
            pl.BlockSpec((None, nh, seq), lambda b: (b, 0, 0)),
            pl.BlockSpec((None, seq, nh), lambda b: (b, 0, 0)),
        ],
        out_shape=[
            jax.ShapeDtypeStruct((batch, nh, seq), F32),
            jax.ShapeDtypeStruct((batch, seq, nh), F32),
        ],
        compiler_params=_cparams(("arbitrary",)),
        name="cumsum",
    )(lfT)


def _attn_kernel(q_ref, kT_ref, v_ref, cum_ref, cumT_ref, o_ref,
                 qop_ref, kop_ref, m_ref, l_ref, al_ref, acc_ref, s_ref, p_ref, *, t):
    first = (pl.program_id(0) == 0) & (pl.program_id(1) == 0) & (pl.program_id(2) == 0)
    i = pl.program_id(1)
    j = pl.program_id(2)
    n_pairs = N_HEADS // 2
    ch = ATTN_CH
    lane = lax.broadcasted_iota(jnp.int32, (1, LANES), 1)
    halves = (lane < HEAD_DIM, lane >= HEAD_DIM)

    @pl.when(first)
    def _():
        kop_ref[...] = jnp.zeros(kop_ref.shape, BF16)

    @pl.when(j == 0)
    def _():
        m_ref[...] = jnp.full(m_ref.shape, NEG_INF, F32)
        l_ref[...] = jnp.zeros(l_ref.shape, F32)
        acc_ref[...] = jnp.zeros(acc_ref.shape, F32)
        parts = [x.astype(F32) for x in _split3(cum_ref[...] * LOG2E)]
        for h in range(N_HEADS):
            e = h % 2
            base = (1 - e) * HEAD_DIM
            hi, mid, lo = (jnp.broadcast_to(x[:, h:h + 1], (t, LANES)) for x in parts)
            ones = jnp.where((lane >= base + 3) & (lane < base + 6), 1.0, 0.0)
            ext = jnp.where(lane == base, hi, jnp.where(lane == base + 1, mid,
                  jnp.where(lane == base + 2, lo, ones)))
            qop_ref[h] = jnp.where(halves[e], q_ref[h // 2].astype(F32), ext).astype(BF16)

    def step(diagonal):
        kparts = [x.astype(F32) for x in _split3(-(cumT_ref[...] * LOG2E))]
        row16 = lax.broadcasted_iota(jnp.int32, (16, t), 0)
        for h in range(N_HEADS):
            e = h % 2
            own = slice(e * HEAD_DIM, (e + 1) * HEAD_DIM)
            base = (1 - e) * HEAD_DIM
            kop_ref[h, own, :] = kT_ref[h // 2, own, :]
            hi, mid, lo = (x[h:h + 1, :] for x in kparts)
            ext = jnp.where(row16 < 3, 1.0, jnp.where(row16 == 3, hi, jnp.where(row16 == 4, mid,
                  jnp.where(row16 == 5, lo, 0.0))))
            kop_ref[h, base:base + 16, :] = ext.astype(BF16)
        if diagonal:
            col_minus_row = (lax.broadcasted_iota(jnp.int32, (ch, LANES), 1)
                             - lax.broadcasted_iota(jnp.int32, (ch, LANES), 0))
        n_ch, n_u = t // ch, t // LANES

        def above(r, u):
            return diagonal and u * LANES > r * ch + ch - 1

        def crosses(r, u):
            return diagonal and not above(r, u) and u * LANES + LANES - 1 > r * ch

        def scores(hp, slot):
            for e in range(2):
                h = 2 * hp + e
                s_ref[slot, e] = _dot(qop_ref[h], kop_ref[h])

        def softmax(hp, slot):
            for e in range(2):
                h = 2 * hp + e
                for r in range(n_ch):
                    rows = slice(r * ch, (r + 1) * ch)
                    mx = None
                    for u in range(n_u):
                        if above(r, u):
                            continue
                        lsl = slice(u * LANES, (u + 1) * LANES)
                        su = s_ref[slot, e, rows, lsl]
                        if crosses(r, u):
                            su = jnp.where(col_minus_row <= (r * ch - u * LANES), su, NEG_INF)
                            s_ref[slot, e, rows, lsl] = su
                        mx = su if mx is None else jnp.maximum(mx, su)
                    m_old = m_ref[h, rows, :]
                    m_new = jnp.maximum(m_old, jnp.max(mx, axis=1, keepdims=True))
                    al_ref[slot, e, rows, :] = jnp.exp2(m_old - m_new)
                    m_ref[h, rows, :] = m_new
                for r in range(n_ch):
                    rows = slice(r * ch, (r + 1) * ch)
                    m_new = m_ref[h, rows, :]
                    psum = None
                    for u in range(n_u):
                        lsl = slice(u * LANES, (u + 1) * LANES)
                        if above(r, u):
                            p_ref[slot, e, rows, lsl] = jnp.zeros((ch, LANES), BF16)
                            continue
                        pu = jnp.exp2(s_ref[slot, e, rows, lsl] - m_new)
                        psum = pu if psum is None else psum + pu
                        p_ref[slot, e, rows, lsl] = pu.astype(BF16)
                    l_ref[h, rows, :] = (al_ref[slot, e, rows, :] * l_ref[h, rows, :]
                                         + jnp.sum(psum, axis=1, keepdims=True))

        def values(hp, slot):
            contrib = [_dot(p_ref[slot, e], v_ref[hp]) for e in range(2)]
            alpha2 = jnp.where(halves[0], al_ref[slot, 0], al_ref[slot, 1])
            acc_ref[hp] = alpha2 * acc_ref[hp] + jnp.where(halves[0], contrib[0], contrib[1])

        scores(0, 0)
        for k in range(n_pairs):
            if k + 1 < n_pairs:
                scores(k + 1, (k + 1) % 2)
            softmax(k, k % 2)
            values(k, k % 2)

    @pl.when(j < i)
    def _():
        step(False)

    @pl.when(j == i)
    def _():
        step(True)
        for hp in range(n_pairs):
            l2 = jnp.where(halves[0], l_ref[2 * hp], l_ref[2 * hp + 1])
            o_ref[:, hp * LANES:(hp + 1) * LANES] = (acc_ref[hp] / l2).astype(BF16)


def _attention(q, kTb, v, cum, cumT, *, batch, seq):
    n_pairs, m, _ = q.shape
    d = N_HEADS * HEAD_DIM
    t = ATTN_T
    nq = seq // t
    qmap = lambda b, i, j: (b * nq + i, 0)
    return pl.pallas_call(
        functools.partial(_attn_kernel, t=t),
        grid=(batch, nq, nq),
        in_specs=[
            pl.BlockSpec((n_pairs, t, LANES), lambda b, i, j: (0, b * nq + i, 0)),
            pl.BlockSpec((None, n_pairs, LANES, t), lambda b, i, j: (b, 0, 0, jnp.minimum(j, i))),
            pl.BlockSpec((n_pairs, t, LANES), lambda b, i, j: (0, b * nq + jnp.minimum(j, i), 0)),
            pl.BlockSpec((t, N_HEADS), qmap),
            pl.BlockSpec((None, N_HEADS, t), lambda b, i, j: (b, 0, jnp.minimum(j, i))),
        ],
        out_specs=pl.BlockSpec((t, d), qmap),
        out_shape=jax.ShapeDtypeStruct((m, d), BF16),
        scratch_shapes=[
            pltpu.VMEM((N_HEADS, t, LANES), BF16),
            pltpu.VMEM((N_HEADS, LANES, t), BF16),
            pltpu.VMEM((N_HEADS, t, LANES), F32),
            pltpu.VMEM((N_HEADS, t, LANES), F32),
            pltpu.VMEM((2, 2, t, LANES), F32),
            pltpu.VMEM((n_pairs, t, LANES), F32),
            pltpu.VMEM((2, 2, t, t), F32),
            pltpu.VMEM((2, 2, t, t), BF16),
        ],
        compiler_params=_cparams(("arbitrary", "arbitrary", "arbitrary")),
        name="attention",
    )(q, kTb.reshape(batch, n_pairs, LANES, seq), v, cum, cumT)


def _decode_kernel(pt_ref, qT_ref, kTs_ref, vTs_ref, lfTs_ref, *rest, n_req, group, n_steps):
    k_refs = rest[:group]
    v_refs = rest[group:2 * group]
    lf_refs = rest[2 * group:3 * group]
    o_ref = rest[3 * group]
    qb_ref, acc_ref, m_ref, l_ref, carry_ref, s_ref, p_ref, ab_ref = rest[3 * group + 1:]
    b = pl.program_id(0)
    j = pl.program_id(1)
    d_model = N_HEADS * HEAD_DIM

    @pl.when(j == 0)
    def _():
        sel = lax.broadcasted_iota(jnp.int32, (1, n_req), 1) == b
        pick = lambda ref: jnp.sum(jnp.where(sel, ref[...], 0.0), axis=1, keepdims=True)
        qcol = pick(qT_ref)
        kcol = pick(kTs_ref)
        vcol = pick(vTs_ref)
        qk = qcol * kcol
        lane0 = lax.broadcasted_iota(jnp.int32, (1, LANES), 1) == 0
        for h in range(N_HEADS):
            hs = slice(h * HEAD_DIM, (h + 1) * HEAD_DIM)
            qb_ref[h] = jnp.broadcast_to(qcol[hs], (HEAD_DIM, LANES))
            acc_ref[h] = jnp.where(lane0, vcol[hs], 0.0)
            m_ref[h:h + 1, :] = jnp.sum(qk[hs], axis=0, keepdims=True)
        l_ref[...] = jnp.ones(l_ref.shape, F32)
        carry_ref[...] = pick(lfTs_ref)

    r = lax.broadcasted_iota(jnp.int32, (PAGE_SIZE, PAGE_SIZE), 0)
    c = lax.broadcasted_iota(jnp.int32, (PAGE_SIZE, PAGE_SIZE), 1)
    after = (r > c).astype(BF16)

    for h in range(N_HEADS):
        qb = qb_ref[h]
        for g in range(group):
            s_ref[g, h:h + 1, :] = jnp.sum(k_refs[g][h] * qb, axis=0, keepdims=True)
    carry = carry_ref[...]
    logits = []
    for g in range(group):
        lfT = lf_refs[g][...]
        logits.append(s_ref[g] + (_dot3(lfT, after) + carry))
        carry = carry + jnp.sum(lfT, axis=1, keepdims=True)
    carry_ref[...] = carry
    m_prev = m_ref[...]
    m_next = jnp.maximum(m_prev, jnp.max(functools.reduce(jnp.maximum, logits), axis=1, keepdims=True))
    alpha = jnp.exp(m_prev - m_next)
    psum = None
    for g in range(group):
        p = jnp.exp(logits[g] - m_next)
        p_ref[g] = p
        psum = p if psum is None else psum + p
    l_ref[...] = alpha * l_ref[...] + jnp.sum(psum, axis=1, keepdims=True)
    m_ref[...] = m_next
    ab_ref[...] = jnp.broadcast_to(alpha, ab_ref.shape)
    for h in range(N_HEADS):
        acc = ab_ref[h:h + 1, :] * acc_ref[h]
        for g in range(group):
            acc = acc + v_refs[g][h] * p_ref[g, h:h + 1, :]
        acc_ref[h] = acc

    @pl.when(j == n_steps - 1)
    def _():
        for h in range(N_HEADS):
            acc_ref[h] = acc_ref[h] / l_ref[h:h + 1, :]
        ones = jnp.ones((8, LANES), BF16)
        o8 = _nt_dot3(ones, acc_ref[...].reshape(d_model, LANES))
        o_ref[...] = o8[0:1, :].astype(BF16)


def _decode_attention(page_table, qT, kTs, vTs, lfTs, kT_cache, vT_cache, lfT_cache):
    n_req, n_pages = page_table.shape
    d = N_HEADS * HEAD_DIM
    group = DEC_G
    n_steps = n_pages // group

    def page(g):
        return lambda b, j, pt: (pt[b, n_pages - 1 - (j * group + g)], 0, 0, 0)

    def page3(g):
        return lambda b, j, pt: (pt[b, n_pages - 1 - (j * group + g)], 0, 0)

    small = lambda b, j, pt: (0, 0)
    kv_block = (None, N_HEADS, HEAD_DIM, PAGE_SIZE)
    in_specs = [
        pl.BlockSpec((d, n_req), small),
        pl.BlockSpec((d, n_req), small),
        pl.BlockSpec((d, n_req), small),
        pl.BlockSpec((N_HEADS, n_req), small),
    ]
    in_specs += [pl.BlockSpec(kv_block, page(g)) for g in range(group)]
    in_specs += [pl.BlockSpec(kv_block, page(g)) for g in range(group)]
    in_specs += [pl.BlockSpec((None, N_HEADS, PAGE_SIZE), page3(g)) for g in range(group)]
    grid_spec = pltpu.PrefetchScalarGridSpec(
        num_scalar_prefetch=1,
        grid=(n_req, n_steps),
        in_specs=in_specs,
        out_specs=pl.BlockSpec((None, 1, d), lambda b, j, pt: (b, 0, 0)),
        scratch_shapes=[
            pltpu.VMEM((N_HEADS, HEAD_DIM, LANES), F32),
            pltpu.VMEM((N_HEADS, HEAD_DIM, LANES), F32),
            pltpu.VMEM((N_HEADS, 1), F32),
            pltpu.VMEM((N_HEADS, 1), F32),
            pltpu.VMEM((N_HEADS, 1), F32),
            pltpu.VMEM((group, N_HEADS, PAGE_SIZE), F32),
            pltpu.VMEM((group, N_HEADS, PAGE_SIZE), F32),
            pltpu.VMEM((N_HEADS, LANES), F32),
        ],
    )
    out = pl.pallas_call(
        functools.partial(_decode_kernel, n_req=n_req, group=group, n_steps=n_steps),
        grid_spec=grid_spec,
        out_shape=jax.ShapeDtypeStruct((n_req, 1, d), BF16),
        compiler_params=_cparams(("arbitrary", "arbitrary")),
        name="decode_attention",
    )(page_table, qT, kTs, vTs, lfTs,
      *([kT_cache] * group), *([vT_cache] * group), *([lfT_cache] * group))
    return out.reshape(n_req, d)


def _ffn_kernel(x_ref, z_ref, wmo_ref, g_ref, wg_ref, wu_ref, wo_ref, gf_ref, out_ref,
                acc_ref, h_ref, *, n_chunks, final_norm):
    c = pl.program_id(1)

    @pl.when(c == 0)
    def _():
        y1 = x_ref[...] + _dot(z_ref[...], wmo_ref[...])
        acc_ref[...] = y1
        h_ref[...] = _rms(y1, g_ref[...]).astype(BF16)

    h = h_ref[...]
    gate = _dot(h, wg_ref[...])
    up = _dot(h, wu_ref[...])
    a = (gate * jax.nn.sigmoid(gate) * up).astype(BF16)
    acc_ref[...] += _dot(a, wo_ref[...])

    @pl.when(c == n_chunks - 1)
    def _():
        y = acc_ref[...]
        out_ref[...] = _rms(y, gf_ref[...]) if final_norm else y


def _mix_ffn(x, z, wmo, g, w_in, w_out, g_final, *, layer, tm, final_norm):
    m, d = x.shape
    d_ff = w_out.shape[1]
    tc = FFN_TC
    n_chunks = d_ff // tc
    row = lambda i, c: (i, 0)
    const = lambda i, c: (0, 0)
    return pl.pallas_call(
        functools.partial(_ffn_kernel, n_chunks=n_chunks, final_norm=final_norm),
        grid=(m // tm, n_chunks),
        in_specs=[
            pl.BlockSpec((tm, d), row),
            pl.BlockSpec((tm, d), row),
            pl.BlockSpec((d, d), const),
            pl.BlockSpec((1, d), const),
            pl.BlockSpec((None, d, tc), lambda i, c: (layer, 0, c)),
            pl.BlockSpec((None, d, tc), lambda i, c: (layer, 0, n_chunks + c)),
            pl.BlockSpec((None, tc, d), lambda i, c: (layer, c, 0)),
            pl.BlockSpec((1, d), const),
        ],
        out_specs=pl.BlockSpec((tm, d), row),
        out_shape=jax.ShapeDtypeStruct((m, d), F32),
        scratch_shapes=[pltpu.VMEM((tm, d), F32), pltpu.VMEM((tm, d), BF16)],
        compiler_params=_cparams(("arbitrary", "arbitrary")),
        name="mix_ffn",
    )(x, z, wmo, g, w_in, w_in, w_out, g_final)


def _conv_prompt_kernel(x_ref, g_ref, wb_ref, wc_ref, wu_ref, ck_ref, prev_ref, z_ref, st_ref,
                        buf_ref, *, tm, blocks_per_seq):
    i = pl.program_id(0)
    h = _rms(x_ref[...], g_ref[...]).astype(BF16)
    gate = _dot(h, wb_ref[...])
    cu = _dot(h, wc_ref[...]) * _dot(h, wu_ref[...])

    @pl.when(i % blocks_per_seq == 0)
    def _():
        buf_ref[6:8, :] = prev_ref[...]

    buf_ref[8:8 + tm, :] = cu
    y = ck_ref[0:1, :] * buf_ref[6:6 + tm, :] + ck_ref[1:2, :] * buf_ref[7:7 + tm, :]
    y = y + ck_ref[2:3, :] * cu
    z_ref[...] = (gate * y).astype(BF16)
    tail = cu[tm - 2:tm, :]
    st_ref[...] = tail
    buf_ref[6:8, :] = tail


def _conv_prompt(x, g, w_in, ck, prev, *, batch, seq):
    m, d = x.shape
    tm = CONV_TM
    bps = seq // tm
    const = lambda i: (0, 0)
    return pl.pallas_call(
        functools.partial(_conv_prompt_kernel, tm=tm, blocks_per_seq=bps),
        grid=(m // tm,),
        in_specs=[
            pl.BlockSpec((tm, d), lambda i: (i, 0)),
            pl.BlockSpec((1, d), const),
            pl.BlockSpec((d, d), lambda i: (0, 0)),
            pl.BlockSpec((d, d), lambda i: (0, 1)),
            pl.BlockSpec((d, d), lambda i: (0, 2)),
            pl.BlockSpec((3, d), const),
            pl.BlockSpec((None, 2, d), lambda i: (i // bps, 0, 0)),
        ],
        out_specs=[
            pl.BlockSpec((tm, d), lambda i: (i, 0)),
            pl.BlockSpec((None, 2, d), lambda i: (i // bps, 0, 0)),
        ],
        out_shape=[
            jax.ShapeDtypeStruct((m, d), BF16),
            jax.ShapeDtypeStruct((batch, 2, d), F32),
        ],
        scratch_shapes=[pltpu.VMEM((tm + 8, d), F32)],
        compiler_params=_cparams(("arbitrary",)),
        name="conv_prompt",
    )(x, g, w_in, w_in, w_in, ck, prev)


def _conv_sample_kernel(x_ref, g_ref, wb_ref, wc_ref, wu_ref, ck_ref, p0_ref, p1_ref, z_ref, cu_ref):
    h = _rms(x_ref[...], g_ref[...]).astype(BF16)
    gate = _dot(h, wb_ref[...])
    cu = _dot(h, wc_ref[...]) * _dot(h, wu_ref[...])
    y = ck_ref[0:1, :] * p0_ref[...] + ck_ref[1:2, :] * p1_ref[...]
    y = y + ck_ref[2:3, :] * cu
    z_ref[...] = (gate * y).astype(BF16)
    cu_ref[...] = cu


def _conv_sample(x, g, w_in, ck, p0, p1):
    n, d = x.shape
    full = lambda shape: pl.BlockSpec(shape, lambda i: (0, 0))
    return pl.pallas_call(
        _conv_sample_kernel,
        grid=(1,),
        in_specs=[
            full((n, d)),
            full((1, d)),
            pl.BlockSpec((d, d), lambda i: (0, 0)),
            pl.BlockSpec((d, d), lambda i: (0, 1)),
            pl.BlockSpec((d, d), lambda i: (0, 2)),
            full((3, d)),
            full((n, d)),
            full((n, d)),
        ],
        out_specs=[full((n, d)), full((n, d))],
        out_shape=[jax.ShapeDtypeStruct((n, d), BF16), jax.ShapeDtypeStruct((n, d), F32)],
        compiler_params=_cparams(("arbitrary",)),
        name="conv_sample",
    )(x, g, w_in, w_in, w_in, ck, p0, p1)


def kernel(x_prompt, x_sample, cache_k, cache_v, cache_logf, state_conv, page_table,
           norm_mix, norm_ffn, w_ffn_in, w_ffn_out, norm_final,
           attn_w_qkvf, attn_b_f, attn_w_o, conv_w_in, conv_kernel, conv_w_out):
    batch, seq, d = x_prompt.shape
    n_req = x_sample.shape[0]
    assert x_sample.shape[1] == 1 and d == N_HEADS * HEAD_DIM
    assert norm_mix.shape[0] == 2 and cache_k.shape[2] == PAGE_SIZE

    xp = x_prompt.reshape(batch * seq, d)
    xs = x_sample.reshape(n_req, d)
    gfin = norm_final.reshape(1, d)

    wT = attn_w_qkvf[0].T.astype(BF16)
    bf = attn_b_f[0].reshape(N_HEADS, 1)
    g0 = norm_mix[0].reshape(1, d)

    q, kT, kTb, v, vT, lfT = _proj_prompt(xp, g0, wT, bf, batch=batch, seq=seq)
    cumT, cum = _cumsum(lfT)
    o_p = _attention(q, kTb, v, cum.reshape(batch * seq, N_HEADS), cumT, batch=batch, seq=seq)

    qTs, kTs, vTs, lfTs = _proj_sample(xs, g0, wT, bf)
    o_s = _decode_attention(
        page_table, qTs, kTs, vTs, lfTs,
        jnp.transpose(cache_k[0], (0, 2, 3, 1)),
        jnp.transpose(cache_v[0], (0, 2, 3, 1)),
        jnp.transpose(cache_logf[0], (0, 2, 1)))

    wo = attn_w_o[0].astype(BF16)
    w_in, w_out = w_ffn_in.astype(BF16), w_ffn_out.astype(BF16)
    gf0 = norm_ffn[0].reshape(1, d)
    yp = _mix_ffn(xp, o_p, wo, gf0, w_in, w_out, gfin, layer=0, tm=FFN_TM, final_norm=False)
    ys = _mix_ffn(xs, o_s, wo, gf0, w_in, w_out, gfin, layer=0, tm=n_req, final_norm=False)

    wci = conv_w_in[0].astype(BF16)
    g1 = norm_mix[1].reshape(1, d)
    ck = conv_kernel[0]
    zp, conv_p = _conv_prompt(yp, g1, wci, ck, jnp.zeros((batch, 2, d), F32), batch=batch, seq=seq)
    zs, cu_s = _conv_sample(ys, g1, wci, ck, state_conv[0, :, 0], state_conv[0, :, 1])

    wco = conv_w_out[0].astype(BF16)
    gf1 = norm_ffn[1].reshape(1, d)
    yp = _mix_ffn(yp, zp, wco, gf1, w_in, w_out, gfin, layer=1, tm=FFN_TM, final_norm=True)
    ys = _mix_ffn(ys, zs, wco, gf1, w_in, w_out, gfin, layer=1, tm=n_req, final_norm=True)

    to_heads = lambda t: jnp.transpose(t.reshape(batch, N_HEADS, HEAD_DIM, seq), (0, 3, 1, 2))[None]
    to_heads_s = lambda t: t.T.reshape(1, n_req, 1, N_HEADS, HEAD_DIM)
    return (
        yp.reshape(batch, seq, d),
        ys.reshape(n_req, 1, d),
        to_heads(kT),
        to_heads(vT),
        jnp.transpose(lfT, (0, 2, 1))[None],
        to_heads_s(kTs),
        to_heads_s(vTs),
        lfTs.T.reshape(1, n_req, 1, N_HEADS),
        conv_p[None],
        jnp.stack([state_conv[0, :, 1], cu_s], axis=1)[None],
    )
```

```python
import functools

import jax
import jax.numpy as jnp
from jax import lax
from jax.experimental import pallas as pl
from jax.experimental.pallas import tpu as pltpu

F32 = jnp.float32
BF16 = jnp.bfloat16

RMS_EPS = 1e-6
NEG_INF = -1e30
N_HEADS = 16
HEAD_DIM = 64
PAGE_SIZE = 128
LANES = 128
VMEM_LIMIT = 56 * 1024 * 1024

PROJ_TM = 512
ATTN_T = 512
ATTN_CH = 64
LOG2E = 1.4426950408889634
FFN_TM = 1024
FFN_TC = 256
CONV_TM = 512
DEC_G = 8


def _cparams(sem):
    return pltpu.CompilerParams(dimension_semantics=sem, vmem_limit_bytes=VMEM_LIMIT)


def _rms(x, g):
    ms = jnp.mean(x * x, axis=-1, keepdims=True)
    return x * lax.rsqrt(ms + RMS_EPS) * g


def _dot(a, b):
    return jnp.dot(a, b, preferred_element_type=F32)


def _nt_dot(a, b):
    return lax.dot_general(a, b, (((1,), (1,)), ((), ())), preferred_element_type=F32)


def _log_sigmoid(x):
    return jnp.minimum(x, 0.0) - jnp.log1p(jnp.exp(-jnp.abs(x)))


def _split3(x):
    hi = x.astype(BF16)
    r1 = x - hi.astype(F32)
    mid = r1.astype(BF16)
    lo = (r1 - mid.astype(F32)).astype(BF16)
    return hi, mid, lo


def _dot3(x, w):
    hi, mid, lo = _split3(x)
    return _dot(hi, w) + _dot(mid, w) + _dot(lo, w)


def _nt_dot3(w, x):
    hi, mid, lo = _split3(x)
    return _nt_dot(w, hi) + _nt_dot(w, mid) + _nt_dot(w, lo)


def _qkvf_specs(d, index):
    return [
        pl.BlockSpec((d, d), index(0)),
        pl.BlockSpec((d, d), index(1)),
        pl.BlockSpec((d, d), index(2)),
        pl.BlockSpec((N_HEADS, d), index(3 * d // N_HEADS)),
    ]


def _proj_prompt_kernel(x_ref, g_ref, wqT_ref, wkT_ref, wvT_ref, wfT_ref, bf_ref,
                        qT_ref, k_ref, kT_ref, vT_ref, vTb_ref, lfT_ref, *, scale):
    h = _rms(x_ref[...], g_ref[...]).astype(BF16)
    qT_ref[...] = (_nt_dot(wqT_ref[...], h) * scale).astype(BF16)
    k = _nt_dot(h, wkT_ref[...]).astype(BF16)
    for hp in range(N_HEADS // 2):
        k_ref[hp] = k[:, hp * LANES:(hp + 1) * LANES]
    kT_ref[...] = _nt_dot(wkT_ref[...], h)
    vT = _nt_dot(wvT_ref[...], h)
    vT_ref[...] = vT
    vTb_ref[...] = vT.astype(BF16)
    lfT_ref[...] = _log_sigmoid(_nt_dot(wfT_ref[...], h) + bf_ref[...])


def _proj_prompt(x, g, wT, bf, *, batch, seq):
    m, d = x.shape
    tm = PROJ_TM
    nb = seq // tm
    n_pairs = N_HEADS // 2
    const = lambda i: (0, 0)
    tmaj = lambda i: (i // nb, 0, i % nb)
    return pl.pallas_call(
        functools.partial(_proj_prompt_kernel, scale=HEAD_DIM ** -0.5 * LOG2E),
        grid=(m // tm,),
        in_specs=[
            pl.BlockSpec((tm, d), lambda i: (i, 0)),
            pl.BlockSpec((1, d), const),
            *_qkvf_specs(d, lambda k: (lambda i: (k, 0))),
            pl.BlockSpec((N_HEADS, 1), const),
        ],
        out_specs=[
            pl.BlockSpec((None, d, tm), tmaj),
            pl.BlockSpec((n_pairs, tm, LANES), lambda i: (0, i, 0)),
            pl.BlockSpec((None, d, tm), tmaj),
            pl.BlockSpec((None, d, tm), tmaj),
            pl.BlockSpec((None, d, tm), tmaj),
            pl.BlockSpec((None, N_HEADS, tm), tmaj),
        ],
        out_shape=[
            jax.ShapeDtypeStruct((batch, d, seq), BF16),
            jax.ShapeDtypeStruct((n_pairs, m, LANES), BF16),
            jax.ShapeDtypeStruct((batch, d, seq), F32),
            jax.ShapeDtypeStruct((batch, d, seq), F32),
            jax.ShapeDtypeStruct((batch, d, seq), BF16),
            jax.ShapeDtypeStruct((batch, N_HEADS, seq), F32),
        ],
        compiler_params=_cparams(("arbitrary",)),
        name="proj_prompt",
    )(x, g, wT, wT, wT, wT, bf)


def _proj_sample_kernel(x_ref, g_ref, wqT_ref, wkT_ref, wvT_ref, wfT_ref, bf_ref,
                        qT_ref, kT_ref, vT_ref, lfT_ref, *, scale):
    h = _rms(x_ref[...], g_ref[...]).astype(BF16)
    qT_ref[...] = _nt_dot(wqT_ref[...], h) * scale
    kT_ref[...] = _nt_dot(wkT_ref[...], h)
    vT_ref[...] = _nt_dot(wvT_ref[...], h)
    lfT_ref[...] = _log_sigmoid(_nt_dot(wfT_ref[...], h) + bf_ref[...])


def _proj_sample(x, g, wT, bf):
    n, d = x.shape
    full = lambda shape: pl.BlockSpec(shape, lambda i: (0, 0))
    return pl.pallas_call(
        functools.partial(_proj_sample_kernel, scale=HEAD_DIM ** -0.5),
        grid=(1,),
        in_specs=[
            full((n, d)),
            full((1, d)),
            *_qkvf_specs(d, lambda k: (lambda i: (k, 0))),
            full((N_HEADS, 1)),
        ],
        out_specs=[full((d, n)), full((d, n)), full((d, n)), full((N_HEADS, n))],
        out_shape=[
            jax.ShapeDtypeStruct((d, n), F32),
            jax.ShapeDtypeStruct((d, n), F32),
            jax.ShapeDtypeStruct((d, n), F32),
            jax.ShapeDtypeStruct((N_HEADS, n), F32),
        ],
        compiler_params=_cparams(("arbitrary",)),
        name="proj_sample",
    )(x, g, wT, wT, wT, wT, bf)


def _cumsum_kernel(lfT_ref, cumT_ref, ka_ref, *, seq):
    blk = LANES
    n_pairs = N_HEADS // 2
    r = lax.broadcasted_iota(jnp.int32, (blk, blk), 0)
    c = lax.broadcasted_iota(jnp.int32, (blk, blk), 1)
    upper = (r <= c).astype(BF16)
    eye = (r == c).astype(BF16)
    head = lax.broadcasted_iota(jnp.int32, (N_HEADS, n_pairs * LANES), 0)
    col = lax.broadcasted_iota(jnp.int32, (N_HEADS, n_pairs * LANES), 1)
    target = (head >> 1) * LANES + 3 + 3 * (head & 1)
    place = [(col == target + part).astype(BF16) for part in range(3)]
    ones_lanes = ((lax.broadcasted_iota(jnp.int32, (1, n_pairs * LANES), 1) & (LANES - 1)) < 3).astype(F32)
    carry = jnp.zeros((N_HEADS, 1), F32)
    for i in range(seq // blk):
        sl = slice(i * blk, (i + 1) * blk)
        cb = _dot3(lfT_ref[:, sl], upper) + carry
        cumT_ref[:, sl] = cb
        carry = cb[:, blk - 1:blk]
        cbt = _nt_dot3(eye, cb)
        hi, mid, lo = _split3(-(cbt * LOG2E))
        ka = (_dot(hi, place[0]) + _dot(mid, place[1]) + _dot(lo, place[2]) + ones_lanes).astype(BF16)
        for hp in range(n_pairs):
            ka_ref[hp, sl, :] = ka[:, hp * LANES:(hp + 1) * LANES]


def _cumsum(lfT):
    batch, nh, seq = lfT.shape
    n_pairs = nh // 2
    return pl.pallas_call(
        functools.partial(_cumsum_kernel, seq=seq),
        grid=(batch,),
        in_specs=[pl.BlockSpec((None, nh, seq), lambda b: (b, 0, 0))],
        out_specs=[
            pl.BlockSpec((None, nh, seq), lambda b: (b, 0, 0)),
            pl.BlockSpec((n_pairs, seq, LANES), lambda b: (0, b, 0)),
        ],
        out_shape=[
            jax.ShapeDtypeStruct((batch, nh, seq), F32),
            jax.ShapeDtypeStruct((n_pairs, batch * seq, LANES), BF16),
        ],
        compiler_params=_cparams(("arbitrary",)),
        name="cumsum",
    )(lfT)


def _attn_kernel(k_ref, ka_ref, qT_ref, vT_ref, cumT_ref, o_ref,
                 qop_ref, m_ref, l_ref, al_ref, acc_ref, s_ref, p_ref, oT_ref, *, t):
    first = (pl.program_id(0) == 0) & (pl.program_id(1) == 0) & (pl.program_id(2) == 0)
    i = pl.program_id(1)
    j = pl.program_id(2)
    n_pairs = N_HEADS // 2
    ch = ATTN_CH
    sub = 8

    @pl.when(first)
    def _():
        qop_ref[...] = jnp.zeros(qop_ref.shape, BF16)

    @pl.when(j == 0)
    def _():
        m_ref[...] = jnp.full(m_ref.shape, NEG_INF, F32)
        l_ref[...] = jnp.zeros(l_ref.shape, F32)
        acc_ref[...] = jnp.zeros(acc_ref.shape, F32)
        parts = [x.astype(F32) for x in _split3(cumT_ref[...] * LOG2E)]
        row16 = lax.broadcasted_iota(jnp.int32, (16, t), 0)
        for h in range(N_HEADS):
            e = h % 2
            own = slice(e * HEAD_DIM, (e + 1) * HEAD_DIM)
            qop_ref[h, own, :] = qT_ref[h // 2, own, :]
            hi, mid, lo = (x[h:h + 1, :] for x in parts)
            ones = jnp.where((row16 >= 3 + 3 * e) & (row16 < 6 + 3 * e), 1.0, 0.0)
            ext = jnp.where(row16 == 0, hi, jnp.where(row16 == 1, mid, jnp.where(row16 == 2, lo, ones)))
            qop_ref[h, LANES:LANES + 16, :] = ext.astype(BF16)

    def step(diagonal):
        if diagonal:
            row_minus_col = (lax.broadcasted_iota(jnp.int32, (ch, LANES), 0)
                             - lax.broadcasted_iota(jnp.int32, (ch, LANES), 1))
        n_ch, n_u = t // ch, t // LANES
        ones_rows = jnp.ones((16, t), BF16)

        def above(r, u):
            return diagonal and r * ch > u * LANES + LANES - 1

        def crosses(r, u):
            return diagonal and not above(r, u) and r * ch + ch - 1 > u * LANES

        def scores(hp, slot):
            kx = jnp.concatenate([k_ref[hp], ka_ref[hp]], axis=1)
            for e in range(2):
                s_ref[slot, e] = _dot(kx, qop_ref[2 * hp + e])

        def softmax(hp, slot):
            for e in range(2):
                h = 2 * hp + e
                for u in range(n_u):
                    lsl = slice(u * LANES, (u + 1) * LANES)
                    mx = None
                    for r in range(n_ch):
                        if above(r, u):
                            continue
                        rows = slice(r * ch, (r + 1) * ch)
                        blk = s_ref[slot, e, rows, lsl]
                        if crosses(r, u):
                            blk = jnp.where(row_minus_col <= (u * LANES - r * ch), blk, NEG_INF)
                            s_ref[slot, e, rows, lsl] = blk
                        part = jnp.max(blk.reshape(ch // sub, sub, LANES), axis=0)
                        mx = part if mx is None else jnp.maximum(mx, part)
                    m_old = m_ref[h, :, lsl]
                    m_new = jnp.maximum(m_old, jnp.max(mx, axis=0, keepdims=True))
                    al_ref[slot, e, :, lsl] = jnp.exp2(m_old - m_new)
                    m_ref[h, :, lsl] = m_new
                    for r in range(n_ch):
                        rows = slice(r * ch, (r + 1) * ch)
                        if above(r, u):
                            p_ref[slot, e, rows, lsl] = jnp.zeros((ch, LANES), BF16)
                            continue
                        blk = s_ref[slot, e, rows, lsl].reshape(ch // sub, sub, LANES)
                        p = jnp.exp2(blk - m_new[None])
                        p_ref[slot, e, rows, lsl] = p.reshape(ch, LANES).astype(BF16)

        def values(hp, slot):
            for e in range(2):
                h = 2 * hp + e
                vx = jnp.concatenate([vT_ref[h], ones_rows], axis=0)
                c = _dot(vx, p_ref[slot, e])
                al = al_ref[slot, e]
                acc = acc_ref[h].reshape(HEAD_DIM // sub, sub, t) * al[None]
                acc_ref[h] = acc.reshape(HEAD_DIM, t) + c[0:HEAD_DIM]
                l_ref[h] = al * l_ref[h] + c[HEAD_DIM:HEAD_DIM + sub]

        scores(0, 0)
        for k in range(n_pairs):
            if k + 1 < n_pairs:
                scores(k + 1, (k + 1) % 2)
            softmax(k, k % 2)
            values(k, k % 2)

    @pl.when(j < i)
    def _():
        step(False)

    @pl.when(j == i)
    def _():
        step(True)
        for h in range(N_HEADS):
            o = acc_ref[h].reshape(HEAD_DIM // sub, sub, t) / l_ref[h][None]
            oT_ref[h * HEAD_DIM:(h + 1) * HEAD_DIM, :] = o.reshape(HEAD_DIM, t)
        o_ref[...] = oT_ref[...].T.astype(BF16)


def _attention(k, ka, qT, vTb, cumT, *, batch, seq):
    n_pairs, m, _ = k.shape
    d = N_HEADS * HEAD_DIM
    t = ATTN_T
    nq = seq // t
    kvrow = lambda b, i, j: (0, b * nq + jnp.minimum(j, i), 0)
    return pl.pallas_call(
        functools.partial(_attn_kernel, t=t),
        grid=(batch, nq, nq),
        in_specs=[
            pl.BlockSpec((n_pairs, t, LANES), kvrow),
            pl.BlockSpec((n_pairs, t, LANES), kvrow),
            pl.BlockSpec((None, n_pairs, LANES, t), lambda b, i, j: (b, 0, 0, i)),
            pl.BlockSpec((None, N_HEADS, HEAD_DIM, t), lambda b, i, j: (b, 0, 0, jnp.minimum(j, i))),
            pl.BlockSpec((None, N_HEADS, t), lambda b, i, j: (b, 0, i)),
        ],
        out_specs=pl.BlockSpec((t, d), lambda b, i, j: (b * nq + i, 0)),
        out_shape=jax.ShapeDtypeStruct((m, d), BF16),
        scratch_shapes=[
            pltpu.VMEM((N_HEADS, 2 * LANES, t), BF16),
            pltpu.VMEM((N_HEADS, 8, t), F32),
            pltpu.VMEM((N_HEADS, 8, t), F32),
            pltpu.VMEM((2, 2, 8, t), F32),
            pltpu.VMEM((N_HEADS, HEAD_DIM, t), F32),
            pltpu.VMEM((2, 2, t, t), F32),
            pltpu.VMEM((2, 2, t, t), BF16),
            pltpu.VMEM((d, t), F32),
        ],
        compiler_params=_cparams(("arbitrary", "arbitrary", "arbitrary")),
        name="attention",
    )(k, ka, qT.reshape(batch, n_pairs, LANES, seq), vTb.reshape(batch, N_HEADS, HEAD_DIM, seq), cumT)


def _decode_kernel(pt_ref, qT_ref, kTs_ref, vTs_ref, lfTs_ref, *rest, n_req, group, n_steps):
    k_refs = rest[:group]
    v_refs = rest[group:2 * group]
    lf_refs = rest[2 * group:3 * group]
    o_ref = rest[3 * group]
    qb_ref, acc_ref, m_ref, l_ref, carry_ref, s_ref, p_ref, ab_ref = rest[3 * group + 1:]
    b = pl.program_id(0)
    j = pl.program_id(1)
    d_model = N_HEADS * HEAD_DIM

    @pl.when(j == 0)
    def _():
        sel = lax.broadcasted_iota(jnp.int32, (1, n_req), 1) == b
        pick = lambda ref: jnp.sum(jnp.where(sel, ref[...], 0.0), axis=1, keepdims=True)
        qcol = pick(qT_ref)
        kcol = pick(kTs_ref)
        vcol = pick(vTs_ref)
        qk = qcol * kcol
        lane0 = lax.broadcasted_iota(jnp.int32, (1, LANES), 1) == 0
        for h in range(N_HEADS):
            hs = slice(h * HEAD_DIM, (h + 1) * HEAD_DIM)
            qb_ref[h] = jnp.broadcast_to(qcol[hs], (HEAD_DIM, LANES))
            acc_ref[h] = jnp.where(lane0, vcol[hs], 0.0)
            m_ref[h:h + 1, :] = jnp.sum(qk[hs], axis=0, keepdims=True)
        l_ref[...] = jnp.ones(l_ref.shape, F32)
        carry_ref[...] = pick(lfTs_ref)

    r = lax.broadcasted_iota(jnp.int32, (PAGE_SIZE, PAGE_SIZE), 0)
    c = lax.broadcasted_iota(jnp.int32, (PAGE_SIZE, PAGE_SIZE), 1)
    after = (r > c).astype(BF16)

    for h in range(N_HEADS):
        qb = qb_ref[h]
        for g in range(group):
            s_ref[g, h:h + 1, :] = jnp.sum(k_refs[g][h] * qb, axis=0, keepdims=True)
    carry = carry_ref[...]
    logits = []
    for g in range(group):
        lfT = lf_refs[g][...]
        logits.append(s_ref[g] + (_dot3(lfT, after) + carry))
        carry = carry + jnp.sum(lfT, axis=1, keepdims=True)
    carry_ref[...] = carry
    m_prev = m_ref[...]
    m_next = jnp.maximum(m_prev, jnp.max(functools.reduce(jnp.maximum, logits), axis=1, keepdims=True))
    alpha = jnp.exp(m_prev - m_next)
    psum = None
    for g in range(group):
        p = jnp.exp(logits[g] - m_next)
        p_ref[g] = p
        psum = p if psum is None else psum + p
    l_ref[...] = alpha * l_ref[...] + jnp.sum(psum, axis=1, keepdims=True)
    m_ref[...] = m_next
    ab_ref[...] = jnp.broadcast_to(alpha, ab_ref.shape)
    for h in range(N_HEADS):
        acc = ab_ref[h:h + 1, :] * acc_ref[h]
        for g in range(group):
            acc = acc + v_refs[g][h] * p_ref[g, h:h + 1, :]
        acc_ref[h] = acc

    @pl.when(j == n_steps - 1)
    def _():
        for h in range(N_HEADS):
            acc_ref[h] = acc_ref[h] / l_ref[h:h + 1, :]
        ones = jnp.ones((8, LANES), BF16)
        o8 = _nt_dot3(ones, acc_ref[...].reshape(d_model, LANES))
        o_ref[...] = o8[0:1, :].astype(BF16)


def _decode_attention(page_table, qT, kTs, vTs, lfTs, kT_cache, vT_cache, lfT_cache):
    n_req, n_pages = page_table.shape
    d = N_HEADS * HEAD_DIM
    group = DEC_G
    n_steps = n_pages // group

    def page(g):
        return lambda b, j, pt: (pt[b, n_pages - 1 - (j * group + g)], 0, 0, 0)

    def page3(g):
        return lambda b, j, pt: (pt[b, n_pages - 1 - (j * group + g)], 0, 0)

    small = lambda b, j, pt: (0, 0)
    kv_block = (None, N_HEADS, HEAD_DIM, PAGE_SIZE)
    in_specs = [
        pl.BlockSpec((d, n_req), small),
        pl.BlockSpec((d, n_req), small),
        pl.BlockSpec((d, n_req), small),
        pl.BlockSpec((N_HEADS, n_req), small),
    ]
    in_specs += [pl.BlockSpec(kv_block, page(g)) for g in range(group)]
    in_specs += [pl.BlockSpec(kv_block, page(g)) for g in range(group)]
    in_specs += [pl.BlockSpec((None, N_HEADS, PAGE_SIZE), page3(g)) for g in range(group)]
    grid_spec = pltpu.PrefetchScalarGridSpec(
        num_scalar_prefetch=1,
        grid=(n_req, n_steps),
        in_specs=in_specs,
        out_specs=pl.BlockSpec((None, 1, d), lambda b, j, pt: (b, 0, 0)),
        scratch_shapes=[
            pltpu.VMEM((N_HEADS, HEAD_DIM, LANES), F32),
            pltpu.VMEM((N_HEADS, HEAD_DIM, LANES), F32),
            pltpu.VMEM((N_HEADS, 1), F32),
            pltpu.VMEM((N_HEADS, 1), F32),
            pltpu.VMEM((N_HEADS, 1), F32),
            pltpu.VMEM((group, N_HEADS, PAGE_SIZE), F32),
            pltpu.VMEM((group, N_HEADS, PAGE_SIZE), F32),
            pltpu.VMEM((N_HEADS, LANES), F32),
        ],
    )
    out = pl.pallas_call(
        functools.partial(_decode_kernel, n_req=n_req, group=group, n_steps=n_steps),
        grid_spec=grid_spec,
        out_shape=jax.ShapeDtypeStruct((n_req, 1, d), BF16),
        compiler_params=_cparams(("arbitrary", "arbitrary")),
        name="decode_attention",
    )(page_table, qT, kTs, vTs, lfTs,
      *([kT_cache] * group), *([vT_cache] * group), *([lfT_cache] * group))
    return out.reshape(n_req, d)


def _ffn_kernel(x_ref, z_ref, wmo_ref, g_ref, wg_ref, wu_ref, wo_ref, gf_ref, out_ref,
                acc_ref, h_ref, *, n_chunks, final_norm):
    c = pl.program_id(1)

    @pl.when(c == 0)
    def _():
        y1 = x_ref[...] + _dot(z_ref[...], wmo_ref[...])
        acc_ref[...] = y1
        h_ref[...] = _rms(y1, g_ref[...]).astype(BF16)

    h = h_ref[...]
    gate = _dot(h, wg_ref[...])
    up = _dot(h, wu_ref[...])
    a = (gate * jax.nn.sigmoid(gate) * up).astype(BF16)
    acc_ref[...] += _dot(a, wo_ref[...])

    @pl.when(c == n_chunks - 1)
    def _():
        y = acc_ref[...]
        out_ref[...] = _rms(y, gf_ref[...]) if final_norm else y


def _mix_ffn(x, z, wmo, g, w_in, w_out, g_final, *, layer, tm, final_norm):
    m, d = x.shape
    d_ff = w_out.shape[1]
    tc = FFN_TC
    n_chunks = d_ff // tc
    row = lambda i, c: (i, 0)
    const = lambda i, c: (0, 0)
    return pl.pallas_call(
        functools.partial(_ffn_kernel, n_chunks=n_chunks, final_norm=final_norm),
        grid=(m // tm, n_chunks),
        in_specs=[
            pl.BlockSpec((tm, d), row),
            pl.BlockSpec((tm, d), row),
            pl.BlockSpec((d, d), const),
            pl.BlockSpec((1, d), const),
            pl.BlockSpec((None, d, tc), lambda i, c: (layer, 0, c)),
            pl.BlockSpec((None, d, tc), lambda i, c: (layer, 0, n_chunks + c)),
            pl.BlockSpec((None, tc, d), lambda i, c: (layer, c, 0)),
            pl.BlockSpec((1, d), const),
        ],
        out_specs=pl.BlockSpec((tm, d), row),
        out_shape=jax.ShapeDtypeStruct((m, d), F32),
        scratch_shapes=[pltpu.VMEM((tm, d), F32), pltpu.VMEM((tm, d), BF16)],
        compiler_params=_cparams(("arbitrary", "arbitrary")),
        name="mix_ffn",
    )(x, z, wmo, g, w_in, w_in, w_out, g_final)


def _conv_prompt_kernel(x_ref, g_ref, wb_ref, wc_ref, wu_ref, ck_ref, prev_ref, z_ref, st_ref,
                        buf_ref, *, tm, blocks_per_seq):
    i = pl.program_id(0)
    h = _rms(x_ref[...], g_ref[...]).astype(BF16)
    gate = _dot(h, wb_ref[...])
    cu = _dot(h, wc_ref[...]) * _dot(h, wu_ref[...])

    @pl.when(i % blocks_per_seq == 0)
    def _():
        buf_ref[6:8, :] = prev_ref[...]

    buf_ref[8:8 + tm, :] = cu
    y = ck_ref[0:1, :] * buf_ref[6:6 + tm, :] + ck_ref[1:2, :] * buf_ref[7:7 + tm, :]
    y = y + ck_ref[2:3, :] * cu
    z_ref[...] = (gate * y).astype(BF16)
    tail = cu[tm - 2:tm, :]
    st_ref[...] = tail
    buf_ref[6:8, :] = tail


def _conv_prompt(x, g, w_in, ck, prev, *, batch, seq):
    m, d = x.shape
    tm = CONV_TM
    bps = seq // tm
    const = lambda i: (0, 0)
    return pl.pallas_call(
        functools.partial(_conv_prompt_kernel, tm=tm, blocks_per_seq=bps),
        grid=(m // tm,),
        in_specs=[
            pl.BlockSpec((tm, d), lambda i: (i, 0)),
            pl.BlockSpec((1, d), const),
            pl.BlockSpec((d, d), lambda i: (0, 0)),
            pl.BlockSpec((d, d), lambda i: (0, 1)),
            pl.BlockSpec((d, d), lambda i: (0, 2)),
            pl.BlockSpec((3, d), const),
            pl.BlockSpec((None, 2, d), lambda i: (i // bps, 0, 0)),
        ],
        out_specs=[
            pl.BlockSpec((tm, d), lambda i: (i, 0)),
            pl.BlockSpec((None, 2, d), lambda i: (i // bps, 0, 0)),
        ],
        out_shape=[
            jax.ShapeDtypeStruct((m, d), BF16),
            jax.ShapeDtypeStruct((batch, 2, d), F32),
        ],
        scratch_shapes=[pltpu.VMEM((tm + 8, d), F32)],
        compiler_params=_cparams(("arbitrary",)),
        name="conv_prompt",
    )(x, g, w_in, w_in, w_in, ck, prev)


def _conv_sample_kernel(x_ref, g_ref, wb_ref, wc_ref, wu_ref, ck_ref, p0_ref, p1_ref, z_ref, cu_ref):
    h = _rms(x_ref[...], g_ref[...]).astype(BF16)
    gate = _dot(h, wb_ref[...])
    cu = _dot(h, wc_ref[...]) * _dot(h, wu_ref[...])
    y = ck_ref[0:1, :] * p0_ref[...] + ck_ref[1:2, :] * p1_ref[...]
    y = y + ck_ref[2:3, :] * cu
    z_ref[...] = (gate * y).astype(BF16)
    cu_ref[...] = cu


def _conv_sample(x, g, w_in, ck, p0, p1):
    n, d = x.shape
    full = lambda shape: pl.BlockSpec(shape, lambda i: (0, 0))
    return pl.pallas_call(
        _conv_sample_kernel,
        grid=(1,),
        in_specs=[
            full((n, d)),
            full((1, d)),
            pl.BlockSpec((d, d), lambda i: (0, 0)),
            pl.BlockSpec((d, d), lambda i: (0, 1)),
            pl.BlockSpec((d, d), lambda i: (0, 2)),
            full((3, d)),
            full((n, d)),
            full((n, d)),
        ],
        out_specs=[full((n, d)), full((n, d))],
        out_shape=[jax.ShapeDtypeStruct((n, d), BF16), jax.ShapeDtypeStruct((n, d), F32)],
        compiler_params=_cparams(("arbitrary",)),
        name="conv_sample",
    )(x, g, w_in, w_in, w_in, ck, p0, p1)


def kernel(x_prompt, x_sample, cache_k, cache_v, cache_logf, state_conv, page_table,
           norm_mix, norm_ffn, w_ffn_in, w_ffn_out, norm_final,
           attn_w_qkvf, attn_b_f, attn_w_o, conv_w_in, conv_kernel, conv_w_out):
    batch, seq, d = x_prompt.shape
    n_req = x_sample.shape[0]
    assert x_sample.shape[1] == 1 and d == N_HEADS * HEAD_DIM
    assert norm_mix.shape[0] == 2 and cache_k.shape[2] == PAGE_SIZE

    xp = x_prompt.reshape(batch * seq, d)
    xs = x_sample.reshape(n_req, d)
    gfin = norm_final.reshape(1, d)

    wT = attn_w_qkvf[0].T.astype(BF16)
    bf = attn_b_f[0].reshape(N_HEADS, 1)
    g0 = norm_mix[0].reshape(1, d)

    qT, k, kT, vT, vTb, lfT = _proj_prompt(xp, g0, wT, bf, batch=batch, seq=seq)
    cumT, ka = _cumsum(lfT)
    o_p = _attention(k, ka, qT, vTb, cumT, batch=batch, seq=seq)

    qTs, kTs, vTs, lfTs = _proj_sample(xs, g0, wT, bf)
    o_s = _decode_attention(
        page_table, qTs, kTs, vTs, lfTs,
        jnp.transpose(cache_k[0], (0, 2, 3, 1)),
        jnp.transpose(cache_v[0], (0, 2, 3, 1)),
        jnp.transpose(cache_logf[0], (0, 2, 1)))

    wo = attn_w_o[0].astype(BF16)
    w_in, w_out = w_ffn_in.astype(BF16), w_ffn_out.astype(BF16)
    gf0 = norm_ffn[0].reshape(1, d)
    yp = _mix_ffn(xp, o_p, wo, gf0, w_in, w_out, gfin, layer=0, tm=FFN_TM, final_norm=False)
    ys = _mix_ffn(xs, o_s, wo, gf0, w_in, w_out, gfin, layer=0, tm=n_req, final_norm=False)

    wci = conv_w_in[0].astype(BF16)
    g1 = norm_mix[1].reshape(1, d)
    ck = conv_kernel[0]
    zp, conv_p = _conv_prompt(yp, g1, wci, ck, jnp.zeros((batch, 2, d), F32), batch=batch, seq=seq)
    zs, cu_s = _conv_sample(ys, g1, wci, ck, state_conv[0, :, 0], state_conv[0, :, 1])

    wco = conv_w_out[0].astype(BF16)
    gf1 = norm_ffn[1].reshape(1, d)
    yp = _mix_ffn(yp, zp, wco, gf1, w_in, w_out, gfin, layer=1, tm=FFN_TM, final_norm=True)
    ys = _mix_ffn(ys, zs, wco, gf1, w_in, w_out, gfin, layer=1, tm=n_req, final_norm=True)

    to_heads = lambda t: jnp.transpose(t.reshape(batch, N_HEADS, HEAD_DIM, seq), (0, 3, 1, 2))[None]
    to_heads_s = lambda t: t.T.reshape(1, n_req, 1, N_HEADS, HEAD_DIM)
    return (
        yp.reshape(batch, seq, d),
        ys.reshape(n_req, 1, d),
        to_heads(kT),
        to_heads(vT),
        jnp.transpose(lfT, (0, 2, 1))[None],
        to_heads_s(kTs),
        to_heads_s(vTs),
        lfTs.T.reshape(1, n_req, 1, N_HEADS),
        conv_p[None],
        jnp.stack([state_conv[0, :, 1], cu_s], axis=1)[None],
    )
```

```python
import functools

import jax
import jax.numpy as jnp
from jax import lax
from jax.experimental import pallas as pl
from jax.experimental.pallas import tpu as pltpu

F32 = jnp.float32
BF16 = jnp.bfloat16

RMS_EPS = 1e-6
NEG_INF = -1e30
N_HEADS = 16
HEAD_DIM = 64
PAGE_SIZE = 128
LANES = 128
VMEM_LIMIT = 56 * 1024 * 1024

PROJ_TM = 512
ATTN_T = 512
ATTN_CH = 64
LOG2E = 1.4426950408889634
FFN_TM = 1024
FFN_TC = 256
CONV_TM = 512
FFN_DEC_G = 8


def _cparams(sem):
    return pltpu.CompilerParams(dimension_semantics=sem, vmem_limit_bytes=VMEM_LIMIT)


def _rms(x, g):
    ms = jnp.mean(x * x, axis=-1, keepdims=True)
    return x * lax.rsqrt(ms + RMS_EPS) * g


def _dot(a, b):
    return jnp.dot(a, b, preferred_element_type=F32)


def _nt_dot(a, b):
    return lax.dot_general(a, b, (((1,), (1,)), ((), ())), preferred_element_type=F32)


def _log_sigmoid(x):
    return jnp.minimum(x, 0.0) - jnp.log1p(jnp.exp(-jnp.abs(x)))


def _split3(x):
    hi = x.astype(BF16)
    r1 = x - hi.astype(F32)
    mid = r1.astype(BF16)
    lo = (r1 - mid.astype(F32)).astype(BF16)
    return hi, mid, lo


def _dot3(x, w):
    hi, mid, lo = _split3(x)
    return _dot(hi, w) + _dot(mid, w) + _dot(lo, w)


def _nt_dot3(w, x):
    hi, mid, lo = _split3(x)
    return _nt_dot(w, hi) + _nt_dot(w, mid) + _nt_dot(w, lo)


def _qkvf_specs(d, index):
    return [
        pl.BlockSpec((d, d), index(0)),
        pl.BlockSpec((d, d), index(1)),
        pl.BlockSpec((d, d), index(2)),
        pl.BlockSpec((N_HEADS, d), index(3 * d // N_HEADS)),
    ]


def _proj_prompt_kernel(x_ref, g_ref, wqT_ref, wkT_ref, wvT_ref, wfT_ref, bf_ref,
                        qT_ref, k_ref, kT_ref, vT_ref, vTb_ref, lfT_ref, *, scale):
    h = _rms(x_ref[...], g_ref[...]).astype(BF16)
    qT_ref[...] = (_nt_dot(wqT_ref[...], h) * scale).astype(BF16)
    k = _nt_dot(h, wkT_ref[...]).astype(BF16)
    for hp in range(N_HEADS // 2):
        k_ref[hp] = k[:, hp * LANES:(hp + 1) * LANES]
    kT_ref[...] = _nt_dot(wkT_ref[...], h)
    vT = _nt_dot(wvT_ref[...], h)
    vT_ref[...] = vT
    vTb_ref[...] = vT.astype(BF16)
    lfT_ref[...] = _log_sigmoid(_nt_dot(wfT_ref[...], h) + bf_ref[...])


def _proj_prompt(x, g, wT, bf, *, batch, seq):
    m, d = x.shape
    tm = PROJ_TM
    nb = seq // tm
    n_pairs = N_HEADS // 2
    const = lambda i: (0, 0)
    tmaj = lambda i: (i // nb, 0, i % nb)
    return pl.pallas_call(
        functools.partial(_proj_prompt_kernel, scale=HEAD_DIM ** -0.5 * LOG2E),
        grid=(m // tm,),
        in_specs=[
            pl.BlockSpec((tm, d), lambda i: (i, 0)),
            pl.BlockSpec((1, d), const),
            *_qkvf_specs(d, lambda k: (lambda i: (k, 0))),
            pl.BlockSpec((N_HEADS, 1), const),
        ],
        out_specs=[
            pl.BlockSpec((None, d, tm), tmaj),
            pl.BlockSpec((n_pairs, tm, LANES), lambda i: (0, i, 0)),
            pl.BlockSpec((None, d, tm), tmaj),
            pl.BlockSpec((None, d, tm), tmaj),
            pl.BlockSpec((None, d, tm), tmaj),
            pl.BlockSpec((None, N_HEADS, tm), tmaj),
        ],
        out_shape=[
            jax.ShapeDtypeStruct((batch, d, seq), BF16),
            jax.ShapeDtypeStruct((n_pairs, m, LANES), BF16),
            jax.ShapeDtypeStruct((batch, d, seq), F32),
            jax.ShapeDtypeStruct((batch, d, seq), F32),
            jax.ShapeDtypeStruct((batch, d, seq), BF16),
            jax.ShapeDtypeStruct((batch, N_HEADS, seq), F32),
        ],
        compiler_params=_cparams(("arbitrary",)),
        name="proj_prompt",
    )(x, g, wT, wT, wT, wT, bf)


def _proj_sample_kernel(x_ref, g_ref, wqT_ref, wkT_ref, wvT_ref, wfT_ref, bf_ref,
                        qT_ref, kT_ref, vT_ref, lfT_ref, *, scale):
    h = _rms(x_ref[...], g_ref[...]).astype(BF16)
    qT_ref[...] = _nt_dot(wqT_ref[...], h) * scale
    kT_ref[...] = _nt_dot(wkT_ref[...], h)
    vT_ref[...] = _nt_dot(wvT_ref[...], h)
    lfT_ref[...] = _log_sigmoid(_nt_dot(wfT_ref[...], h) + bf_ref[...])


def _proj_sample(x, g, wT, bf):
    n, d = x.shape
    full = lambda shape: pl.BlockSpec(shape, lambda i: (0, 0))
    return pl.pallas_call(
        functools.partial(_proj_sample_kernel, scale=HEAD_DIM ** -0.5),
        grid=(1,),
        in_specs=[
            full((n, d)),
            full((1, d)),
            *_qkvf_specs(d, lambda k: (lambda i: (k, 0))),
            full((N_HEADS, 1)),
        ],
        out_specs=[full((d, n)), full((d, n)), full((d, n)), full((N_HEADS, n))],
        out_shape=[
            jax.ShapeDtypeStruct((d, n), F32),
            jax.ShapeDtypeStruct((d, n), F32),
            jax.ShapeDtypeStruct((d, n), F32),
            jax.ShapeDtypeStruct((N_HEADS, n), F32),
        ],
        compiler_params=_cparams(("arbitrary",)),
        name="proj_sample",
    )(x, g, wT, wT, wT, wT, bf)


def _cumsum_kernel(lfT_ref, cumT_ref, ka_ref, *, seq):
    blk = LANES
    n_pairs = N_HEADS // 2
    r = lax.broadcasted_iota(jnp.int32, (blk, blk), 0)
    c = lax.broadcasted_iota(jnp.int32, (blk, blk), 1)
    upper = (r <= c).astype(BF16)
    eye = (r == c).astype(BF16)
    head = lax.broadcasted_iota(jnp.int32, (N_HEADS, n_pairs * LANES), 0)
    col = lax.broadcasted_iota(jnp.int32, (N_HEADS, n_pairs * LANES), 1)
    target = (head >> 1) * LANES + 3 + 3 * (head & 1)
    place = [(col == target + part).astype(BF16) for part in range(3)]
    ones_lanes = ((lax.broadcasted_iota(jnp.int32, (1, n_pairs * LANES), 1) & (LANES - 1)) < 3).astype(F32)
    carry = jnp.zeros((N_HEADS, 1), F32)
    for i in range(seq // blk):
        sl = slice(i * blk, (i + 1) * blk)
        cb = _dot3(lfT_ref[:, sl], upper) + carry
        cumT_ref[:, sl] = cb
        carry = cb[:, blk - 1:blk]
        cbt = _nt_dot3(eye, cb)
        hi, mid, lo = _split3(-(cbt * LOG2E))
        ka = (_dot(hi, place[0]) + _dot(mid, place[1]) + _dot(lo, place[2]) + ones_lanes).astype(BF16)
        for hp in range(n_pairs):
            ka_ref[hp, sl, :] = ka[:, hp * LANES:(hp + 1) * LANES]


def _cumsum(lfT):
    batch, nh, seq = lfT.shape
    n_pairs = nh // 2
    return pl.pallas_call(
        functools.partial(_cumsum_kernel, seq=seq),
        grid=(batch,),
        in_specs=[pl.BlockSpec((None, nh, seq), lambda b: (b, 0, 0))],
        out_specs=[
            pl.BlockSpec((None, nh, seq), lambda b: (b, 0, 0)),
            pl.BlockSpec((n_pairs, seq, LANES), lambda b: (0, b, 0)),
        ],
        out_shape=[
            jax.ShapeDtypeStruct((batch, nh, seq), F32),
            jax.ShapeDtypeStruct((n_pairs, batch * seq, LANES), BF16),
        ],
        compiler_params=_cparams(("arbitrary",)),
        name="cumsum",
    )(lfT)


def _attn_kernel(k_ref, ka_ref, qT_ref, vT_ref, cumT_ref, o_ref,
                 qop_ref, m_ref, l_ref, al_ref, acc_ref, s_ref, p_ref, oT_ref, *, t):
    first = (pl.program_id(0) == 0) & (pl.program_id(1) == 0) & (pl.program_id(2) == 0)
    i = pl.program_id(1)
    j = pl.program_id(2)
    n_pairs = N_HEADS // 2
    ch = ATTN_CH
    sub = 8

    @pl.when(first)
    def _():
        qop_ref[...] = jnp.zeros(qop_ref.shape, BF16)

    @pl.when(j == 0)
    def _():
        m_ref[...] = jnp.full(m_ref.shape, NEG_INF, F32)
        l_ref[...] = jnp.zeros(l_ref.shape, F32)
        acc_ref[...] = jnp.zeros(acc_ref.shape, F32)
        parts = [x.astype(F32) for x in _split3(cumT_ref[...] * LOG2E)]
        row16 = lax.broadcasted_iota(jnp.int32, (16, t), 0)
        for h in range(N_HEADS):
            e = h % 2
            own = slice(e * HEAD_DIM, (e + 1) * HEAD_DIM)
            qop_ref[h, own, :] = qT_ref[h // 2, own, :]
            hi, mid, lo = (x[h:h + 1, :] for x in parts)
            ones = jnp.where((row16 >= 3 + 3 * e) & (row16 < 6 + 3 * e), 1.0, 0.0)
            ext = jnp.where(row16 == 0, hi, jnp.where(row16 == 1, mid, jnp.where(row16 == 2, lo, ones)))
            qop_ref[h, LANES:LANES + 16, :] = ext.astype(BF16)

    def step(diagonal):
        if diagonal:
            row_minus_col = (lax.broadcasted_iota(jnp.int32, (ch, LANES), 0)
                             - lax.broadcasted_iota(jnp.int32, (ch, LANES), 1))
        n_ch, n_u = t // ch, t // LANES
        ones_rows = jnp.ones((16, t), BF16)

        def above(r, u):
            return diagonal and r * ch > u * LANES + LANES - 1

        def crosses(r, u):
            return diagonal and not above(r, u) and r * ch + ch - 1 > u * LANES

        def scores(hp, slot):
            kx = jnp.concatenate([k_ref[hp], ka_ref[hp]], axis=1)
            for e in range(2):
                s_ref[slot, e] = _dot(kx, qop_ref[2 * hp + e])

        def softmax(hp, slot):
            for e in range(2):
                h = 2 * hp + e
                for u in range(n_u):
                    lsl = slice(u * LANES, (u + 1) * LANES)
                    mx = None
                    for r in range(n_ch):
                        if above(r, u):
                            continue
                        rows = slice(r * ch, (r + 1) * ch)
                        blk = s_ref[slot, e, rows, lsl]
                        if crosses(r, u):
                            blk = jnp.where(row_minus_col <= (u * LANES - r * ch), blk, NEG_INF)
                            s_ref[slot, e, rows, lsl] = blk
                        part = jnp.max(blk.reshape(ch // sub, sub, LANES), axis=0)
                        mx = part if mx is None else jnp.maximum(mx, part)
                    m_old = m_ref[h, :, lsl]
                    m_new = jnp.maximum(m_old, jnp.max(mx, axis=0, keepdims=True))
                    al_ref[slot, e, :, lsl] = jnp.exp2(m_old - m_new)
                    m_ref[h, :, lsl] = m_new
                    for r in range(n_ch):
                        rows = slice(r * ch, (r + 1) * ch)
                        if above(r, u):
                            p_ref[slot, e, rows, lsl] = jnp.zeros((ch, LANES), BF16)
                            continue
                        blk = s_ref[slot, e, rows, lsl].reshape(ch // sub, sub, LANES)
                        p = jnp.exp2(blk - m_new[None])
                        p_ref[slot, e, rows, lsl] = p.reshape(ch, LANES).astype(BF16)

        def values(hp, slot):
            for e in range(2):
                h = 2 * hp + e
                vx = jnp.concatenate([vT_ref[h], ones_rows], axis=0)
                c = _dot(vx, p_ref[slot, e])
                al = al_ref[slot, e]
                acc = acc_ref[h].reshape(HEAD_DIM // sub, sub, t) * al[None]
                acc_ref[h] = acc.reshape(HEAD_DIM, t) + c[0:HEAD_DIM]
                l_ref[h] = al * l_ref[h] + c[HEAD_DIM:HEAD_DIM + sub]

        scores(0, 0)
        for k in range(n_pairs):
            if k + 1 < n_pairs:
                scores(k + 1, (k + 1) % 2)
            softmax(k, k % 2)
            values(k, k % 2)

    @pl.when(j < i)
    def _():
        step(False)

    @pl.when(j == i)
    def _():
        step(True)
        for h in range(N_HEADS):
            o = acc_ref[h].reshape(HEAD_DIM // sub, sub, t) / l_ref[h][None]
            oT_ref[h * HEAD_DIM:(h + 1) * HEAD_DIM, :] = o.reshape(HEAD_DIM, t)
        o_ref[...] = oT_ref[...].T.astype(BF16)


def _attention(k, ka, qT, vTb, cumT, *, batch, seq):
    n_pairs, m, _ = k.shape
    d = N_HEADS * HEAD_DIM
    t = ATTN_T
    nq = seq // t
    kvrow = lambda b, i, j: (0, b * nq + jnp.minimum(j, i), 0)
    return pl.pallas_call(
        functools.partial(_attn_kernel, t=t),
        grid=(batch, nq, nq),
        in_specs=[
            pl.BlockSpec((n_pairs, t, LANES), kvrow),
            pl.BlockSpec((n_pairs, t, LANES), kvrow),
            pl.BlockSpec((None, n_pairs, LANES, t), lambda b, i, j: (b, 0, 0, i)),
            pl.BlockSpec((None, N_HEADS, HEAD_DIM, t), lambda b, i, j: (b, 0, 0, jnp.minimum(j, i))),
            pl.BlockSpec((None, N_HEADS, t), lambda b, i, j: (b, 0, i)),
        ],
        out_specs=pl.BlockSpec((t, d), lambda b, i, j: (b * nq + i, 0)),
        out_shape=jax.ShapeDtypeStruct((m, d), BF16),
        scratch_shapes=[
            pltpu.VMEM((N_HEADS, 2 * LANES, t), BF16),
            pltpu.VMEM((N_HEADS, 8, t), F32),
            pltpu.VMEM((N_HEADS, 8, t), F32),
            pltpu.VMEM((2, 2, 8, t), F32),
            pltpu.VMEM((N_HEADS, HEAD_DIM, t), F32),
            pltpu.VMEM((2, 2, t, t), F32),
            pltpu.VMEM((2, 2, t, t), BF16),
            pltpu.VMEM((d, t), F32),
        ],
        compiler_params=_cparams(("arbitrary", "arbitrary", "arbitrary")),
        name="attention",
    )(k, ka, qT.reshape(batch, n_pairs, LANES, seq), vTb.reshape(batch, N_HEADS, HEAD_DIM, seq), cumT)


def _decode_init(b, n_req, qT_ref, kTs_ref, vTs_ref, lfTs_ref, state):
    qb_ref, acc_ref, m_ref, l_ref, carry_ref = state[:5]
    sel = lax.broadcasted_iota(jnp.int32, (1, n_req), 1) == b
    pick = lambda ref: jnp.sum(jnp.where(sel, ref[...], 0.0), axis=1, keepdims=True)
    qcol = pick(qT_ref)
    kcol = pick(kTs_ref)
    vcol = pick(vTs_ref)
    qk = qcol * kcol
    lane0 = lax.broadcasted_iota(jnp.int32, (1, LANES), 1) == 0
    for h in range(N_HEADS):
        hs = slice(h * HEAD_DIM, (h + 1) * HEAD_DIM)
        qb_ref[h] = jnp.broadcast_to(qcol[hs], (HEAD_DIM, LANES))
        acc_ref[h] = jnp.where(lane0, vcol[hs], 0.0)
        m_ref[h:h + 1, :] = jnp.sum(qk[hs], axis=0, keepdims=True)
    l_ref[...] = jnp.ones(l_ref.shape, F32)
    carry_ref[...] = pick(lfTs_ref)


def _decode_pages(k_refs, v_refs, lf_refs, state):
    qb_ref, acc_ref, m_ref, l_ref, carry_ref, s_ref, p_ref, ab_ref = state
    group = len(k_refs)
    r = lax.broadcasted_iota(jnp.int32, (PAGE_SIZE, PAGE_SIZE), 0)
    c = lax.broadcasted_iota(jnp.int32, (PAGE_SIZE, PAGE_SIZE), 1)
    after = (r > c).astype(BF16)
    for h in range(N_HEADS):
        qb = qb_ref[h]
        for g in range(group):
            s_ref[g, h:h + 1, :] = jnp.sum(k_refs[g][h] * qb, axis=0, keepdims=True)
    carry = carry_ref[...]
    logits = []
    for g in range(group):
        lfT = lf_refs[g][...]
        logits.append(s_ref[g] + (_dot3(lfT, after) + carry))
        carry = carry + jnp.sum(lfT, axis=1, keepdims=True)
    carry_ref[...] = carry
    m_prev = m_ref[...]
    m_next = jnp.maximum(m_prev, jnp.max(functools.reduce(jnp.maximum, logits), axis=1, keepdims=True))
    alpha = jnp.exp(m_prev - m_next)
    psum = None
    for g in range(group):
        p = jnp.exp(logits[g] - m_next)
        p_ref[g] = p
        psum = p if psum is None else psum + p
    l_ref[...] = alpha * l_ref[...] + jnp.sum(psum, axis=1, keepdims=True)
    m_ref[...] = m_next
    ab_ref[...] = jnp.broadcast_to(alpha, ab_ref.shape)
    for h in range(N_HEADS):
        acc = ab_ref[h:h + 1, :] * acc_ref[h]
        for g in range(group):
            acc = acc + v_refs[g][h] * p_ref[g, h:h + 1, :]
        acc_ref[h] = acc


def _decode_finish(o_ref, state):
    acc_ref, l_ref = state[1], state[3]
    for h in range(N_HEADS):
        acc_ref[h] = acc_ref[h] / l_ref[h:h + 1, :]
    ones = jnp.ones((8, LANES), BF16)
    o8 = _nt_dot3(ones, acc_ref[...].reshape(N_HEADS * HEAD_DIM, LANES))
    o_ref[...] = o8[0:1, :].astype(BF16)


def _decode_scratch(group):
    return [
        pltpu.VMEM((N_HEADS, HEAD_DIM, LANES), F32),
        pltpu.VMEM((N_HEADS, HEAD_DIM, LANES), F32),
        pltpu.VMEM((N_HEADS, 1), F32),
        pltpu.VMEM((N_HEADS, 1), F32),
        pltpu.VMEM((N_HEADS, 1), F32),
        pltpu.VMEM((group, N_HEADS, PAGE_SIZE), F32),
        pltpu.VMEM((group, N_HEADS, PAGE_SIZE), F32),
        pltpu.VMEM((N_HEADS, LANES), F32),
    ]


def _decode_specs(n_req, group, page_of):
    d = N_HEADS * HEAD_DIM
    small = lambda *a: (0, 0)
    kv_block = (None, N_HEADS, HEAD_DIM, PAGE_SIZE)
    kv_map = lambda g: (lambda *a: (page_of(g, *a), 0, 0, 0))
    lf_map = lambda g: (lambda *a: (page_of(g, *a), 0, 0))
    specs = [pl.BlockSpec((d, n_req), small)] * 3 + [pl.BlockSpec((N_HEADS, n_req), small)]
    specs += [pl.BlockSpec(kv_block, kv_map(g)) for g in range(group)]
    specs += [pl.BlockSpec(kv_block, kv_map(g)) for g in range(group)]
    specs += [pl.BlockSpec((None, N_HEADS, PAGE_SIZE), lf_map(g)) for g in range(group)]
    return specs


def _ffn_kernel(*refs, n_chunks, final_norm, dec):
    if dec is None:
        x_ref, z_ref, wmo_ref, g_ref, wg_ref, wu_ref, wo_ref, gf_ref, out_ref, acc_ref, h_ref = refs
    else:
        n_req, group, spr = dec
        x_ref, z_ref, wmo_ref, g_ref, wg_ref, wu_ref, wo_ref, gf_ref = refs[1:9]
        qT_ref, kTs_ref, vTs_ref, lfTs_ref = refs[9:13]
        k_refs = refs[13:13 + group]
        v_refs = refs[13 + group:13 + 2 * group]
        lf_refs = refs[13 + 2 * group:13 + 3 * group]
        out_ref, o_dec_ref, acc_ref, h_ref = refs[13 + 3 * group:17 + 3 * group]
        state = refs[17 + 3 * group:]
        step = pl.program_id(0) * n_chunks + pl.program_id(1)
    c = pl.program_id(1)

    @pl.when(c == 0)
    def _():
        y1 = x_ref[...] + _dot(z_ref[...], wmo_ref[...])
        acc_ref[...] = y1
        h_ref[...] = _rms(y1, g_ref[...]).astype(BF16)

    if dec is not None:
        active = step < n_req * spr

        @pl.when((step % spr == 0) & active)
        def _():
            _decode_init(step // spr, n_req, qT_ref, kTs_ref, vTs_ref, lfTs_ref, state)

    def chunk():
        h = h_ref[...]
        gate = _dot(h, wg_ref[...])
        up = _dot(h, wu_ref[...])
        a = (gate * jax.nn.sigmoid(gate) * up).astype(BF16)
        acc_ref[...] += _dot(a, wo_ref[...])

    if dec is None:
        chunk()
    else:
        @pl.when(active)
        def _():
            chunk()
            _decode_pages(k_refs, v_refs, lf_refs, state)

        @pl.when(jnp.logical_not(active))
        def _():
            chunk()

    @pl.when(c == n_chunks - 1)
    def _():
        y = acc_ref[...]
        out_ref[...] = _rms(y, gf_ref[...]) if final_norm else y

    if dec is not None:
        @pl.when((step % spr == spr - 1) & active)
        def _():
            _decode_finish(o_dec_ref, state)


def _mix_ffn(x, z, wmo, g, w_in, w_out, g_final, *, layer, tm, final_norm, decode=None):
    m, d = x.shape
    d_ff = w_out.shape[1]
    tc = FFN_TC
    n_chunks = d_ff // tc
    row = lambda i, c, *pt: (i, 0)
    const = lambda i, c, *pt: (0, 0)
    in_specs = [
        pl.BlockSpec((tm, d), row),
        pl.BlockSpec((tm, d), row),
        pl.BlockSpec((d, d), const),
        pl.BlockSpec((1, d), const),
        pl.BlockSpec((None, d, tc), lambda i, c, *pt: (layer, 0, c)),
        pl.BlockSpec((None, d, tc), lambda i, c, *pt: (layer, 0, n_chunks + c)),
        pl.BlockSpec((None, tc, d), lambda i, c, *pt: (layer, c, 0)),
        pl.BlockSpec((1, d), const),
    ]
    scratch = [pltpu.VMEM((tm, d), F32), pltpu.VMEM((tm, d), BF16)]
    operands = (x, z, wmo, g, w_in, w_in, w_out, g_final)
    grid = (m // tm, n_chunks)
    y_spec = pl.BlockSpec((tm, d), row)
    y_shape = jax.ShapeDtypeStruct((m, d), F32)
    if decode is None:
        return pl.pallas_call(
            functools.partial(_ffn_kernel, n_chunks=n_chunks, final_norm=final_norm, dec=None),
            grid=grid, in_specs=in_specs, out_specs=y_spec, out_shape=y_shape,
            scratch_shapes=scratch,
            compiler_params=_cparams(("arbitrary", "arbitrary")),
            name="mix_ffn",
        )(*operands)

    page_table, qT, kTs, vTs, lfTs, kT_cache, vT_cache, lfT_cache = decode
    n_req, n_pages = page_table.shape
    group = FFN_DEC_G
    spr = n_pages // group
    assert grid[0] * grid[1] >= n_req * spr, "not enough grid steps for these requests"

    def request_of(i, c):
        step = i * n_chunks + c
        valid = step < n_req * spr
        return jnp.where(valid, step // spr, n_req - 1), jnp.where(valid, step % spr, spr - 1)

    def page_of(g, i, c, pt):
        req, part = request_of(i, c)
        return pt[req, n_pages - 1 - (part * group + g)]

    grid_spec = pltpu.PrefetchScalarGridSpec(
        num_scalar_prefetch=1,
        grid=grid,
        in_specs=in_specs + _decode_specs(n_req, group, page_of),
        out_specs=[y_spec, pl.BlockSpec((None, 1, d), lambda i, c, pt: (request_of(i, c)[0], 0, 0))],
        scratch_shapes=scratch + _decode_scratch(group),
    )
    y, o_dec = pl.pallas_call(
        functools.partial(_ffn_kernel, n_chunks=n_chunks, final_norm=final_norm, dec=(n_req, group, spr)),
        grid_spec=grid_spec,
        out_shape=[y_shape, jax.ShapeDtypeStruct((n_req, 1, d), BF16)],
        compiler_params=_cparams(("arbitrary", "arbitrary")),
        name="mix_ffn_decode",
    )(page_table, *operands, qT, kTs, vTs, lfTs,
      *([kT_cache] * group), *([vT_cache] * group), *([lfT_cache] * group))
    return y, o_dec.reshape(n_req, d)


def _conv_prompt_kernel(x_ref, g_ref, wb_ref, wc_ref, wu_ref, ck_ref, prev_ref, z_ref, st_ref,
                        buf_ref, *, tm, blocks_per_seq):
    i = pl.program_id(0)
    h = _rms(x_ref[...], g_ref[...]).astype(BF16)
    gate = _dot(h, wb_ref[...])
    cu = _dot(h, wc_ref[...]) * _dot(h, wu_ref[...])

    @pl.when(i % blocks_per_seq == 0)
    def _():
        buf_ref[6:8, :] = prev_ref[...]

    buf_ref[8:8 + tm, :] = cu
    y = ck_ref[0:1, :] * buf_ref[6:6 + tm, :] + ck_ref[1:2, :] * buf_ref[7:7 + tm, :]
    y = y + ck_ref[2:3, :] * cu
    z_ref[...] = (gate * y).astype(BF16)
    tail = cu[tm - 2:tm, :]
    st_ref[...] = tail
    buf_ref[6:8, :] = tail


def _conv_prompt(x, g, w_in, ck, prev, *, batch, seq):
    m, d = x.shape
    tm = CONV_TM
    bps = seq // tm
    const = lambda i: (0, 0)
    return pl.pallas_call(
        functools.partial(_conv_prompt_kernel, tm=tm, blocks_per_seq=bps),
        grid=(m // tm,),
        in_specs=[
            pl.BlockSpec((tm, d), lambda i: (i, 0)),
            pl.BlockSpec((1, d), const),
            pl.BlockSpec((d, d), lambda i: (0, 0)),
            pl.BlockSpec((d, d), lambda i: (0, 1)),
            pl.BlockSpec((d, d), lambda i: (0, 2)),
            pl.BlockSpec((3, d), const),
            pl.BlockSpec((None, 2, d), lambda i: (i // bps, 0, 0)),
        ],
        out_specs=[
            pl.BlockSpec((tm, d), lambda i: (i, 0)),
            pl.BlockSpec((None, 2, d), lambda i: (i // bps, 0, 0)),
        ],
        out_shape=[
            jax.ShapeDtypeStruct((m, d), BF16),
            jax.ShapeDtypeStruct((batch, 2, d), F32),
        ],
        scratch_shapes=[pltpu.VMEM((tm + 8, d), F32)],
        compiler_params=_cparams(("arbitrary",)),
        name="conv_prompt",
    )(x, g, w_in, w_in, w_in, ck, prev)


def _conv_sample_kernel(x_ref, g_ref, wb_ref, wc_ref, wu_ref, ck_ref, p0_ref, p1_ref, z_ref, cu_ref):
    h = _rms(x_ref[...], g_ref[...]).astype(BF16)
    gate = _dot(h, wb_ref[...])
    cu = _dot(h, wc_ref[...]) * _dot(h, wu_ref[...])
    y = ck_ref[0:1, :] * p0_ref[...] + ck_ref[1:2, :] * p1_ref[...]
    y = y + ck_ref[2:3, :] * cu
    z_ref[...] = (gate * y).astype(BF16)
    cu_ref[...] = cu


def _conv_sample(x, g, w_in, ck, p0, p1):
    n, d = x.shape
    full = lambda shape: pl.BlockSpec(shape, lambda i: (0, 0))
    return pl.pallas_call(
        _conv_sample_kernel,
        grid=(1,),
        in_specs=[
            full((n, d)),
            full((1, d)),
            pl.BlockSpec((d, d), lambda i: (0, 0)),
            pl.BlockSpec((d, d), lambda i: (0, 1)),
            pl.BlockSpec((d, d), lambda i: (0, 2)),
            full((3, d)),
            full((n, d)),
            full((n, d)),
        ],
        out_specs=[full((n, d)), full((n, d))],
        out_shape=[jax.ShapeDtypeStruct((n, d), BF16), jax.ShapeDtypeStruct((n, d), F32)],
        compiler_params=_cparams(("arbitrary",)),
        name="conv_sample",
    )(x, g, w_in, w_in, w_in, ck, p0, p1)


def kernel(x_prompt, x_sample, cache_k, cache_v, cache_logf, state_conv, page_table,
           norm_mix, norm_ffn, w_ffn_in, w_ffn_out, norm_final,
           attn_w_qkvf, attn_b_f, attn_w_o, conv_w_in, conv_kernel, conv_w_out):
    batch, seq, d = x_prompt.shape
    n_req = x_sample.shape[0]
    assert x_sample.shape[1] == 1 and d == N_HEADS * HEAD_DIM
    assert norm_mix.shape[0] == 2 and cache_k.shape[2] == PAGE_SIZE

    xp = x_prompt.reshape(batch * seq, d)
    xs = x_sample.reshape(n_req, d)
    gfin = norm_final.reshape(1, d)

    wT = attn_w_qkvf[0].T.astype(BF16)
    bf = attn_b_f[0].reshape(N_HEADS, 1)
    g0 = norm_mix[0].reshape(1, d)

    qT, k, kT, vT, vTb, lfT = _proj_prompt(xp, g0, wT, bf, batch=batch, seq=seq)
    cumT, ka = _cumsum(lfT)
    o_p = _attention(k, ka, qT, vTb, cumT, batch=batch, seq=seq)

    qTs, kTs, vTs, lfTs = _proj_sample(xs, g0, wT, bf)
    caches = (jnp.transpose(cache_k[0], (0, 2, 3, 1)),
              jnp.transpose(cache_v[0], (0, 2, 3, 1)),
              jnp.transpose(cache_logf[0], (0, 2, 1)))
    n_pages = page_table.shape[1]
    ffn_steps = (batch * seq // FFN_TM) * (w_ffn_out.shape[1] // FFN_TC)
    n_ride = min(n_req, ffn_steps // (n_pages // FFN_DEC_G))
    assert n_req <= 2 * n_ride

    def requests(lo, hi):
        return (page_table[lo:hi], qTs[:, lo:hi], kTs[:, lo:hi], vTs[:, lo:hi], lfTs[:, lo:hi], *caches)

    wo = attn_w_o[0].astype(BF16)
    w_in, w_out = w_ffn_in.astype(BF16), w_ffn_out.astype(BF16)
    gf0 = norm_ffn[0].reshape(1, d)
    yp, o_s0 = _mix_ffn(xp, o_p, wo, gf0, w_in, w_out, gfin, layer=0, tm=FFN_TM, final_norm=False,
                        decode=requests(0, n_ride))

    wci = conv_w_in[0].astype(BF16)
    g1 = norm_mix[1].reshape(1, d)
    ck = conv_kernel[0]
    wco = conv_w_out[0].astype(BF16)
    gf1 = norm_ffn[1].reshape(1, d)
    zp, conv_p = _conv_prompt(yp, g1, wci, ck, jnp.zeros((batch, 2, d), F32), batch=batch, seq=seq)
    yp, o_s1 = _mix_ffn(yp, zp, wco, gf1, w_in, w_out, gfin, layer=1, tm=FFN_TM, final_norm=True,
                        decode=requests(n_ride, n_req))

    o_s = jnp.concatenate([o_s0, o_s1], axis=0)
    ys = _mix_ffn(xs, o_s, wo, gf0, w_in, w_out, gfin, layer=0, tm=n_req, final_norm=False)
    zs, cu_s = _conv_sample(ys, g1, wci, ck, state_conv[0, :, 0], state_conv[0, :, 1])
    ys = _mix_ffn(ys, zs, wco, gf1, w_in, w_out, gfin, layer=1, tm=n_req, final_norm=True)

    to_heads = lambda t: jnp.transpose(t.reshape(batch, N_HEADS, HEAD_DIM, seq), (0, 3, 1, 2))[None]
    to_heads_s = lambda t: t.T.reshape(1, n_req, 1, N_HEADS, HEAD_DIM)
    return (
        yp.reshape(batch, seq, d),
        ys.reshape(n_req, 1, d),
        to_heads(kT),
        to_heads(vT),
        jnp.transpose(lfT, (0, 2, 1))[None],
        to_heads_s(kTs),
        to_heads_s(vTs),
        lfTs.T.reshape(1, n_req, 1, N_HEADS),
        conv_p[None],
        jnp.stack([state_conv[0, :, 1], cu_s], axis=1)[None],
    )
```

```python
import functools

import jax
import jax.numpy as jnp
from jax import lax
from jax.experimental import pallas as pl
from jax.experimental.pallas import tpu as pltpu

F32 = jnp.float32
BF16 = jnp.bfloat16

RMS_EPS = 1e-6
NEG_INF = -1e30
N_HEADS = 16
HEAD_DIM = 64
PAGE_SIZE = 128
LANES = 128
VMEM_LIMIT = 56 * 1024 * 1024

PROJ_TM = 512
ATTN_T = 512
ATTN_CH = 64
LOG2E = 1.4426950408889634
FFN_TM = 1024
FFN_TC = 256
CONV_TM = 512
FFN_DEC_G = 8


def _cparams(sem):
    return pltpu.CompilerParams(dimension_semantics=sem, vmem_limit_bytes=VMEM_LIMIT)


def _rms(x, g):
    ms = jnp.mean(x * x, axis=-1, keepdims=True)
    return x * lax.rsqrt(ms + RMS_EPS) * g


def _dot(a, b):
    return jnp.dot(a, b, preferred_element_type=F32)


def _nt_dot(a, b):
    return lax.dot_general(a, b, (((1,), (1,)), ((), ())), preferred_element_type=F32)


def _log_sigmoid(x):
    return jnp.minimum(x, 0.0) - jnp.log1p(jnp.exp(-jnp.abs(x)))


def _split3(x):
    hi = x.astype(BF16)
    r1 = x - hi.astype(F32)
    mid = r1.astype(BF16)
    lo = (r1 - mid.astype(F32)).astype(BF16)
    return hi, mid, lo


def _dot3(x, w):
    hi, mid, lo = _split3(x)
    return _dot(hi, w) + _dot(mid, w) + _dot(lo, w)


def _nt_dot3(w, x):
    hi, mid, lo = _split3(x)
    return _nt_dot(w, hi) + _nt_dot(w, mid) + _nt_dot(w, lo)


def _qkvf_specs(d, index):
    return [
        pl.BlockSpec((d, d), index(0)),
        pl.BlockSpec((d, d), index(1)),
        pl.BlockSpec((d, d), index(2)),
        pl.BlockSpec((N_HEADS, d), index(3 * d // N_HEADS)),
    ]


def _proj_prompt_kernel(x_ref, g_ref, wqT_ref, wkT_ref, wvT_ref, wfT_ref, bf_ref,
                        qT_ref, k_ref, kT_ref, vT_ref, vTb_ref, lfT_ref, *, scale):
    h = _rms(x_ref[...], g_ref[...]).astype(BF16)
    qT_ref[...] = (_nt_dot(wqT_ref[...], h) * scale).astype(BF16)
    kT = _nt_dot(wkT_ref[...], h)
    kT_ref[...] = kT
    k = kT.T.astype(BF16)
    for hp in range(N_HEADS // 2):
        k_ref[hp] = k[:, hp * LANES:(hp + 1) * LANES]
    vT = _nt_dot(wvT_ref[...], h)
    vT_ref[...] = vT
    vTb_ref[...] = vT.astype(BF16)
    lfT_ref[...] = _log_sigmoid(_nt_dot(wfT_ref[...], h) + bf_ref[...])


def _proj_prompt(x, g, wT, bf, *, batch, seq):
    m, d = x.shape
    tm = PROJ_TM
    nb = seq // tm
    n_pairs = N_HEADS // 2
    const = lambda i: (0, 0)
    tmaj = lambda i: (i // nb, 0, i % nb)
    return pl.pallas_call(
        functools.partial(_proj_prompt_kernel, scale=HEAD_DIM ** -0.5 * LOG2E),
        grid=(m // tm,),
        in_specs=[
            pl.BlockSpec((tm, d), lambda i: (i, 0)),
            pl.BlockSpec((1, d), const),
            *_qkvf_specs(d, lambda k: (lambda i: (k, 0))),
            pl.BlockSpec((N_HEADS, 1), const),
        ],
        out_specs=[
            pl.BlockSpec((None, d, tm), tmaj),
            pl.BlockSpec((n_pairs, tm, LANES), lambda i: (0, i, 0)),
            pl.BlockSpec((None, d, tm), tmaj),
            pl.BlockSpec((None, d, tm), tmaj),
            pl.BlockSpec((None, d, tm), tmaj),
            pl.BlockSpec((None, N_HEADS, tm), tmaj),
        ],
        out_shape=[
            jax.ShapeDtypeStruct((batch, d, seq), BF16),
            jax.ShapeDtypeStruct((n_pairs, m, LANES), BF16),
            jax.ShapeDtypeStruct((batch, d, seq), F32),
            jax.ShapeDtypeStruct((batch, d, seq), F32),
            jax.ShapeDtypeStruct((batch, d, seq), BF16),
            jax.ShapeDtypeStruct((batch, N_HEADS, seq), F32),
        ],
        compiler_params=_cparams(("arbitrary",)),
        name="proj_prompt",
    )(x, g, wT, wT, wT, wT, bf)


def _proj_sample_kernel(x_ref, g_ref, wqT_ref, wkT_ref, wvT_ref, wfT_ref, bf_ref,
                        qT_ref, kT_ref, vT_ref, lfT_ref, *, scale):
    h = _rms(x_ref[...], g_ref[...]).astype(BF16)
    qT_ref[...] = _nt_dot(wqT_ref[...], h) * scale
    kT_ref[...] = _nt_dot(wkT_ref[...], h)
    vT_ref[...] = _nt_dot(wvT_ref[...], h)
    lfT_ref[...] = _log_sigmoid(_nt_dot(wfT_ref[...], h) + bf_ref[...])


def _proj_sample(x, g, wT, bf):
    n, d = x.shape
    full = lambda shape: pl.BlockSpec(shape, lambda i: (0, 0))
    return pl.pallas_call(
        functools.partial(_proj_sample_kernel, scale=HEAD_DIM ** -0.5),
        grid=(1,),
        in_specs=[
            full((n, d)),
            full((1, d)),
            *_qkvf_specs(d, lambda k: (lambda i: (k, 0))),
            full((N_HEADS, 1)),
        ],
        out_specs=[full((d, n)), full((d, n)), full((d, n)), full((N_HEADS, n))],
        out_shape=[
            jax.ShapeDtypeStruct((d, n), F32),
            jax.ShapeDtypeStruct((d, n), F32),
            jax.ShapeDtypeStruct((d, n), F32),
            jax.ShapeDtypeStruct((N_HEADS, n), F32),
        ],
        compiler_params=_cparams(("arbitrary",)),
        name="proj_sample",
    )(x, g, wT, wT, wT, wT, bf)


def _cumsum_kernel(lfT_ref, cumT_ref, ka_ref, *, seq):
    blk = LANES
    n_pairs = N_HEADS // 2
    r = lax.broadcasted_iota(jnp.int32, (blk, blk), 0)
    c = lax.broadcasted_iota(jnp.int32, (blk, blk), 1)
    upper = (r <= c).astype(BF16)
    eye = (r == c).astype(BF16)
    head = lax.broadcasted_iota(jnp.int32, (N_HEADS, n_pairs * LANES), 0)
    col = lax.broadcasted_iota(jnp.int32, (N_HEADS, n_pairs * LANES), 1)
    target = (head >> 1) * LANES + 3 + 3 * (head & 1)
    place = [(col == target + part).astype(BF16) for part in range(3)]
    ones_lanes = ((lax.broadcasted_iota(jnp.int32, (1, n_pairs * LANES), 1) & (LANES - 1)) < 3).astype(F32)
    carry = jnp.zeros((N_HEADS, 1), F32)
    for i in range(seq // blk):
        sl = slice(i * blk, (i + 1) * blk)
        cb = _dot3(lfT_ref[:, sl], upper) + carry
        cumT_ref[:, sl] = cb
        carry = cb[:, blk - 1:blk]
        cbt = _nt_dot3(eye, cb)
        hi, mid, lo = _split3(-(cbt * LOG2E))
        ka = (_dot(hi, place[0]) + _dot(mid, place[1]) + _dot(lo, place[2]) + ones_lanes).astype(BF16)
        for hp in range(n_pairs):
            ka_ref[hp, sl, :] = ka[:, hp * LANES:(hp + 1) * LANES]


def _cumsum(lfT):
    batch, nh, seq = lfT.shape
    n_pairs = nh // 2
    return pl.pallas_call(
        functools.partial(_cumsum_kernel, seq=seq),
        grid=(batch,),
        in_specs=[pl.BlockSpec((None, nh, seq), lambda b: (b, 0, 0))],
        out_specs=[
            pl.BlockSpec((None, nh, seq), lambda b: (b, 0, 0)),
            pl.BlockSpec((n_pairs, seq, LANES), lambda b: (0, b, 0)),
        ],
        out_shape=[
            jax.ShapeDtypeStruct((batch, nh, seq), F32),
            jax.ShapeDtypeStruct((n_pairs, batch * seq, LANES), BF16),
        ],
        compiler_params=_cparams(("arbitrary",)),
        name="cumsum",
    )(lfT)


def _attn_kernel(qi_ref, kj_ref, k_ref, ka_ref, qT_ref, vT_ref, cumT_ref, o_ref,
                 qop_ref, m_ref, l_ref, al_ref, acc_ref, s_ref, p_ref, oT_ref, *, t):
    first = (pl.program_id(0) == 0) & (pl.program_id(1) == 0)
    i = qi_ref[pl.program_id(1)]
    j = kj_ref[pl.program_id(1)]
    n_pairs = N_HEADS // 2
    ch = ATTN_CH
    sub = 8

    @pl.when(first)
    def _():
        qop_ref[...] = jnp.zeros(qop_ref.shape, BF16)

    @pl.when(j == 0)
    def _():
        m_ref[...] = jnp.full(m_ref.shape, NEG_INF, F32)
        l_ref[...] = jnp.zeros(l_ref.shape, F32)
        acc_ref[...] = jnp.zeros(acc_ref.shape, F32)
        parts = [x.astype(F32) for x in _split3(cumT_ref[...] * LOG2E)]
        row16 = lax.broadcasted_iota(jnp.int32, (16, t), 0)
        for h in range(N_HEADS):
            e = h % 2
            own = slice(e * HEAD_DIM, (e + 1) * HEAD_DIM)
            qop_ref[h, own, :] = qT_ref[h // 2, own, :]
            hi, mid, lo = (x[h:h + 1, :] for x in parts)
            ones = jnp.where((row16 >= 3 + 3 * e) & (row16 < 6 + 3 * e), 1.0, 0.0)
            ext = jnp.where(row16 == 0, hi, jnp.where(row16 == 1, mid, jnp.where(row16 == 2, lo, ones)))
            qop_ref[h, LANES:LANES + 16, :] = ext.astype(BF16)

    def step(diagonal):
        if diagonal:
            row_minus_col = (lax.broadcasted_iota(jnp.int32, (ch, LANES), 0)
                             - lax.broadcasted_iota(jnp.int32, (ch, LANES), 1))
        n_ch, n_u = t // ch, t // LANES
        ones_rows = jnp.ones((16, t), BF16)

        def above(r, u):
            return diagonal and r * ch > u * LANES + LANES - 1

        def crosses(r, u):
            return diagonal and not above(r, u) and r * ch + ch - 1 > u * LANES

        def scores(hp, slot):
            kx = jnp.concatenate([k_ref[hp], ka_ref[hp]], axis=1)
            for e in range(2):
                s_ref[slot, e] = _dot(kx, qop_ref[2 * hp + e])

        def softmax(hp, slot):
            for e in range(2):
                h = 2 * hp + e
                for u in range(n_u):
                    lsl = slice(u * LANES, (u + 1) * LANES)
                    mx = None
                    for r in range(n_ch):
                        if above(r, u):
                            continue
                        rows = slice(r * ch, (r + 1) * ch)
                        blk = s_ref[slot, e, rows, lsl]
                        if crosses(r, u):
                            blk = jnp.where(row_minus_col <= (u * LANES - r * ch), blk, NEG_INF)
                            s_ref[slot, e, rows, lsl] = blk
                        part = jnp.max(blk.reshape(ch // sub, sub, LANES), axis=0)
                        mx = part if mx is None else jnp.maximum(mx, part)
                    m_old = m_ref[h, :, lsl]
                    m_new = jnp.maximum(m_old, jnp.max(mx, axis=0, keepdims=True))
                    al_ref[slot, e, :, lsl] = jnp.exp2(m_old - m_new)
                    m_ref[h, :, lsl] = m_new
                    for r in range(n_ch):
                        rows = slice(r * ch, (r + 1) * ch)
                        if above(r, u):
                            p_ref[slot, e, rows, lsl] = jnp.zeros((ch, LANES), BF16)
                            continue
                        blk = s_ref[slot, e, rows, lsl].reshape(ch // sub, sub, LANES)
                        p = jnp.exp2(blk - m_new[None])
                        p_ref[slot, e, rows, lsl] = p.reshape(ch, LANES).astype(BF16)

        def values(hp, slot):
            for e in range(2):
                h = 2 * hp + e
                vx = jnp.concatenate([vT_ref[h], ones_rows], axis=0)
                c = _dot(vx, p_ref[slot, e])
                al = al_ref[slot, e]
                acc = acc_ref[h].reshape(HEAD_DIM // sub, sub, t) * al[None]
                acc_ref[h] = acc.reshape(HEAD_DIM, t) + c[0:HEAD_DIM]
                l_ref[h] = al * l_ref[h] + c[HEAD_DIM:HEAD_DIM + sub]

        scores(0, 0)
        for k in range(n_pairs):
            if k + 1 < n_pairs:
                scores(k + 1, (k + 1) % 2)
            softmax(k, k % 2)
            values(k, k % 2)

    @pl.when(j < i)
    def _():
        step(False)

    @pl.when(j == i)
    def _():
        step(True)
        for h in range(N_HEADS):
            o = acc_ref[h].reshape(HEAD_DIM // sub, sub, t) / l_ref[h][None]
            oT_ref[h * HEAD_DIM:(h + 1) * HEAD_DIM, :] = o.reshape(HEAD_DIM, t)
        o_ref[...] = oT_ref[...].T.astype(BF16)


def _attention(k, ka, qT, vTb, cumT, *, batch, seq):
    n_pairs, m, _ = k.shape
    d = N_HEADS * HEAD_DIM
    t = ATTN_T
    nq = seq // t
    pairs = [(i, j) for i in range(nq) for j in range(i + 1)]
    qi = jnp.array([i for i, _ in pairs], jnp.int32)
    kj = jnp.array([j for _, j in pairs], jnp.int32)
    kvrow = lambda b, s, qi, kj: (0, b * nq + kj[s], 0)
    grid_spec = pltpu.PrefetchScalarGridSpec(
        num_scalar_prefetch=2,
        grid=(batch, len(pairs)),
        in_specs=[
            pl.BlockSpec((n_pairs, t, LANES), kvrow),
            pl.BlockSpec((n_pairs, t, LANES), kvrow),
            pl.BlockSpec((None, n_pairs, LANES, t), lambda b, s, qi, kj: (b, 0, 0, qi[s])),
            pl.BlockSpec((None, N_HEADS, HEAD_DIM, t), lambda b, s, qi, kj: (b, 0, 0, kj[s])),
            pl.BlockSpec((None, N_HEADS, t), lambda b, s, qi, kj: (b, 0, qi[s])),
        ],
        out_specs=pl.BlockSpec((t, d), lambda b, s, qi, kj: (b * nq + qi[s], 0)),
        scratch_shapes=[
            pltpu.VMEM((N_HEADS, 2 * LANES, t), BF16),
            pltpu.VMEM((N_HEADS, 8, t), F32),
            pltpu.VMEM((N_HEADS, 8, t), F32),
            pltpu.VMEM((2, 2, 8, t), F32),
            pltpu.VMEM((N_HEADS, HEAD_DIM, t), F32),
            pltpu.VMEM((2, 2, t, t), F32),
            pltpu.VMEM((2, 2, t, t), BF16),
            pltpu.VMEM((d, t), F32),
        ],
    )
    return pl.pallas_call(
        functools.partial(_attn_kernel, t=t),
        grid_spec=grid_spec,
        out_shape=jax.ShapeDtypeStruct((m, d), BF16),
        compiler_params=_cparams(("arbitrary", "arbitrary")),
        name="attention",
    )(qi, kj, k, ka, qT.reshape(batch, n_pairs, LANES, seq), vTb.reshape(batch, N_HEADS, HEAD_DIM, seq), cumT)


def _decode_init(b, n_req, qT_ref, kTs_ref, vTs_ref, lfTs_ref, state):
    qb_ref, acc_ref, m_ref, l_ref, carry_ref = state[:5]
    sel = lax.broadcasted_iota(jnp.int32, (1, n_req), 1) == b
    pick = lambda ref: jnp.sum(jnp.where(sel, ref[...], 0.0), axis=1, keepdims=True)
    qcol = pick(qT_ref)
    kcol = pick(kTs_ref)
    vcol = pick(vTs_ref)
    qk = qcol * kcol
    lane0 = lax.broadcasted_iota(jnp.int32, (1, LANES), 1) == 0
    for h in range(N_HEADS):
        hs = slice(h * HEAD_DIM, (h + 1) * HEAD_DIM)
        qb_ref[h] = jnp.broadcast_to(qcol[hs], (HEAD_DIM, LANES))
        acc_ref[h] = jnp.where(lane0, vcol[hs], 0.0)
        m_ref[h:h + 1, :] = jnp.sum(qk[hs], axis=0, keepdims=True)
    l_ref[...] = jnp.ones(l_ref.shape, F32)
    carry_ref[...] = pick(lfTs_ref)


def _decode_scores(k_refs, state):
    qb_ref, s_ref = state[0], state[5]
    for h in range(N_HEADS):
        qb = qb_ref[h]
        for g, k_ref in enumerate(k_refs):
            s_ref[g, h:h + 1, :] = jnp.sum(k_ref[h] * qb, axis=0, keepdims=True)


def _decode_probs(lf_refs, state):
    _, _, m_ref, l_ref, carry_ref, s_ref, p_ref, ab_ref = state
    r = lax.broadcasted_iota(jnp.int32, (PAGE_SIZE, PAGE_SIZE), 0)
    c = lax.broadcasted_iota(jnp.int32, (PAGE_SIZE, PAGE_SIZE), 1)
    after = (r > c).astype(BF16)
    carry = carry_ref[...]
    logits = []
    for g, lf_ref in enumerate(lf_refs):
        lfT = lf_ref[...]
        logits.append(s_ref[g] + (_dot3(lfT, after) + carry))
        carry = carry + jnp.sum(lfT, axis=1, keepdims=True)
    carry_ref[...] = carry
    m_prev = m_ref[...]
    m_next = jnp.maximum(m_prev, jnp.max(functools.reduce(jnp.maximum, logits), axis=1, keepdims=True))
    alpha = jnp.exp(m_prev - m_next)
    psum = None
    for g in range(len(lf_refs)):
        p = jnp.exp(logits[g] - m_next)
        p_ref[g] = p
        psum = p if psum is None else psum + p
    l_ref[...] = alpha * l_ref[...] + jnp.sum(psum, axis=1, keepdims=True)
    m_ref[...] = m_next
    ab_ref[...] = jnp.broadcast_to(alpha, ab_ref.shape)


def _decode_values(v_refs, state):
    acc_ref, p_ref, ab_ref = state[1], state[6], state[7]
    for h in range(N_HEADS):
        acc = ab_ref[h:h + 1, :] * acc_ref[h]
        for g, v_ref in enumerate(v_refs):
            acc = acc + v_ref[h] * p_ref[g, h:h + 1, :]
        acc_ref[h] = acc


def _decode_finish(o_ref, state):
    acc_ref, l_ref = state[1], state[3]
    for h in range(N_HEADS):
        acc_ref[h] = acc_ref[h] / l_ref[h:h + 1, :]
    ones = jnp.ones((8, LANES), BF16)
    o8 = _nt_dot3(ones, acc_ref[...].reshape(N_HEADS * HEAD_DIM, LANES))
    o_ref[...] = o8[0:1, :].astype(BF16)


def _decode_scratch(group):
    return [
        pltpu.VMEM((N_HEADS, HEAD_DIM, LANES), F32),
        pltpu.VMEM((N_HEADS, HEAD_DIM, LANES), F32),
        pltpu.VMEM((N_HEADS, 1), F32),
        pltpu.VMEM((N_HEADS, 1), F32),
        pltpu.VMEM((N_HEADS, 1), F32),
        pltpu.VMEM((group, N_HEADS, PAGE_SIZE), F32),
        pltpu.VMEM((group, N_HEADS, PAGE_SIZE), F32),
        pltpu.VMEM((N_HEADS, LANES), F32),
    ]


def _decode_specs(n_req, group, page_of):
    d = N_HEADS * HEAD_DIM
    small = lambda *a: (0, 0)
    kv_block = (None, N_HEADS, HEAD_DIM, PAGE_SIZE)
    kv_map = lambda g: (lambda *a: (page_of(g, *a), 0, 0, 0))
    lf_map = lambda g: (lambda *a: (page_of(g, *a), 0, 0))
    specs = [pl.BlockSpec((d, n_req), small)] * 3 + [pl.BlockSpec((N_HEADS, n_req), small)]
    specs += [pl.BlockSpec(kv_block, kv_map(g)) for g in range(group)]
    specs += [pl.BlockSpec(kv_block, kv_map(g)) for g in range(group)]
    specs += [pl.BlockSpec((None, N_HEADS, PAGE_SIZE), lf_map(g)) for g in range(group)]
    return specs


def _ffn_kernel(*refs, n_chunks, final_norm, dec):
    if dec is None:
        x_ref, z_ref, wmo_ref, g_ref, wg_ref, wu_ref, wo_ref, gf_ref, out_ref, acc_ref, h_ref = refs
    else:
        n_req, group, spr = dec
        x_ref, z_ref, wmo_ref, g_ref, wg_ref, wu_ref, wo_ref, gf_ref = refs[1:9]
        qT_ref, kTs_ref, vTs_ref, lfTs_ref = refs[9:13]
        k_refs = refs[13:13 + group]
        v_refs = refs[13 + group:13 + 2 * group]
        lf_refs = refs[13 + 2 * group:13 + 3 * group]
        out_ref, o_dec_ref, acc_ref, h_ref = refs[13 + 3 * group:17 + 3 * group]
        state = refs[17 + 3 * group:]
        step = pl.program_id(0) * n_chunks + pl.program_id(1)
    c = pl.program_id(1)

    @pl.when(c == 0)
    def _():
        y1 = x_ref[...] + _dot(z_ref[...], wmo_ref[...])
        acc_ref[...] = y1
        h_ref[...] = _rms(y1, g_ref[...]).astype(BF16)

    if dec is not None:
        active = step < n_req * spr

        @pl.when((step % spr == 0) & active)
        def _():
            _decode_init(step // spr, n_req, qT_ref, kTs_ref, vTs_ref, lfTs_ref, state)

    def chunk(with_pages):
        if with_pages:
            _decode_scores(k_refs, state)
            _decode_probs(lf_refs, state)
            _decode_values(v_refs, state)
        h = h_ref[...]
        gate = _dot(h, wg_ref[...])
        up = _dot(h, wu_ref[...])
        a = (gate * jax.nn.sigmoid(gate) * up).astype(BF16)
        acc_ref[...] += _dot(a, wo_ref[...])

    if dec is None:
        chunk(False)
    else:
        pl.when(active)(lambda: chunk(True))
        pl.when(jnp.logical_not(active))(lambda: chunk(False))

    @pl.when(c == n_chunks - 1)
    def _():
        y = acc_ref[...]
        out_ref[...] = _rms(y, gf_ref[...]) if final_norm else y

    if dec is not None:
        @pl.when((step % spr == spr - 1) & active)
        def _():
            _decode_finish(o_dec_ref, state)


def _mix_ffn(x, z, wmo, g, w_in, w_out, g_final, *, layer, tm, final_norm, decode=None):
    m, d = x.shape
    d_ff = w_out.shape[1]
    tc = FFN_TC
    n_chunks = d_ff // tc
    row = lambda i, c, *pt: (i, 0)
    const = lambda i, c, *pt: (0, 0)
    in_specs = [
        pl.BlockSpec((tm, d), row),
        pl.BlockSpec((tm, d), row),
        pl.BlockSpec((d, d), const),
        pl.BlockSpec((1, d), const),
        pl.BlockSpec((None, d, tc), lambda i, c, *pt: (layer, 0, c)),
        pl.BlockSpec((None, d, tc), lambda i, c, *pt: (layer, 0, n_chunks + c)),
        pl.BlockSpec((None, tc, d), lambda i, c, *pt: (layer, c, 0)),
        pl.BlockSpec((1, d), const),
    ]
    scratch = [pltpu.VMEM((tm, d), F32), pltpu.VMEM((tm, d), BF16)]
    operands = (x, z, wmo, g, w_in, w_in, w_out, g_final)
    grid = (m // tm, n_chunks)
    y_spec = pl.BlockSpec((tm, d), row)
    y_shape = jax.ShapeDtypeStruct((m, d), F32)
    if decode is None:
        return pl.pallas_call(
            functools.partial(_ffn_kernel, n_chunks=n_chunks, final_norm=final_norm, dec=None),
            grid=grid, in_specs=in_specs, out_specs=y_spec, out_shape=y_shape,
            scratch_shapes=scratch,
            compiler_params=_cparams(("arbitrary", "arbitrary")),
            name="mix_ffn",
        )(*operands)

    page_table, qT, kTs, vTs, lfTs, kT_cache, vT_cache, lfT_cache = decode
    n_req, n_pages = page_table.shape
    group = FFN_DEC_G
    spr = n_pages // group
    assert grid[0] * grid[1] >= n_req * spr, "not enough grid steps for these requests"

    def request_of(i, c):
        step = i * n_chunks + c
        valid = step < n_req * spr
        return jnp.where(valid, step // spr, n_req - 1), jnp.where(valid, step % spr, spr - 1)

    def page_of(g, i, c, pt):
        req, part = request_of(i, c)
        return pt[req, n_pages - 1 - (part * group + g)]

    grid_spec = pltpu.PrefetchScalarGridSpec(
        num_scalar_prefetch=1,
        grid=grid,
        in_specs=in_specs + _decode_specs(n_req, group, page_of),
        out_specs=[y_spec, pl.BlockSpec((None, 1, d), lambda i, c, pt: (request_of(i, c)[0], 0, 0))],
        scratch_shapes=scratch + _decode_scratch(group),
    )
    y, o_dec = pl.pallas_call(
        functools.partial(_ffn_kernel, n_chunks=n_chunks, final_norm=final_norm, dec=(n_req, group, spr)),
        grid_spec=grid_spec,
        out_shape=[y_shape, jax.ShapeDtypeStruct((n_req, 1, d), BF16)],
        compiler_params=_cparams(("arbitrary", "arbitrary")),
        name="mix_ffn_decode",
    )(page_table, *operands, qT, kTs, vTs, lfTs,
      *([kT_cache] * group), *([vT_cache] * group), *([lfT_cache] * group))
    return y, o_dec.reshape(n_req, d)


def _conv_prompt_kernel(x_ref, g_ref, wb_ref, wc_ref, wu_ref, ck_ref, prev_ref, z_ref, st_ref,
                        buf_ref, *, tm, blocks_per_seq):
    i = pl.program_id(0)
    h = _rms(x_ref[...], g_ref[...]).astype(BF16)
    gate = _dot(h, wb_ref[...])
    cu = _dot(h, wc_ref[...]) * _dot(h, wu_ref[...])

    @pl.when(i % blocks_per_seq == 0)
    def _():
        buf_ref[6:8, :] = prev_ref[...]

    buf_ref[8:8 + tm, :] = cu
    y = ck_ref[0:1, :] * buf_ref[6:6 + tm, :] + ck_ref[1:2, :] * buf_ref[7:7 + tm, :]
    y = y + ck_ref[2:3, :] * cu
    z_ref[...] = (gate * y).astype(BF16)
    tail = cu[tm - 2:tm, :]
    st_ref[...] = tail
    buf_ref[6:8, :] = tail


def _conv_prompt(x, g, w_in, ck, prev, *, batch, seq):
    m, d = x.shape
    tm = CONV_TM
    bps = seq // tm
    const = lambda i: (0, 0)
    return pl.pallas_call(
        functools.partial(_conv_prompt_kernel, tm=tm, blocks_per_seq=bps),
        grid=(m // tm,),
        in_specs=[
            pl.BlockSpec((tm, d), lambda i: (i, 0)),
            pl.BlockSpec((1, d), const),
            pl.BlockSpec((d, d), lambda i: (0, 0)),
            pl.BlockSpec((d, d), lambda i: (0, 1)),
            pl.BlockSpec((d, d), lambda i: (0, 2)),
            pl.BlockSpec((3, d), const),
            pl.BlockSpec((None, 2, d), lambda i: (i // bps, 0, 0)),
        ],
        out_specs=[
            pl.BlockSpec((tm, d), lambda i: (i, 0)),
            pl.BlockSpec((None, 2, d), lambda i: (i // bps, 0, 0)),
        ],
        out_shape=[
            jax.ShapeDtypeStruct((m, d), BF16),
            jax.ShapeDtypeStruct((batch, 2, d), F32),
        ],
        scratch_shapes=[pltpu.VMEM((tm + 8, d), F32)],
        compiler_params=_cparams(("arbitrary",)),
        name="conv_prompt",
    )(x, g, w_in, w_in, w_in, ck, prev)


def _conv_sample_kernel(x_ref, g_ref, wb_ref, wc_ref, wu_ref, ck_ref, p0_ref, p1_ref, z_ref, cu_ref):
    h = _rms(x_ref[...], g_ref[...]).astype(BF16)
    gate = _dot(h, wb_ref[...])
    cu = _dot(h, wc_ref[...]) * _dot(h, wu_ref[...])
    y = ck_ref[0:1, :] * p0_ref[...] + ck_ref[1:2, :] * p1_ref[...]
    y = y + ck_ref[2:3, :] * cu
    z_ref[...] = (gate * y).astype(BF16)
    cu_ref[...] = cu


def _conv_sample(x, g, w_in, ck, p0, p1):
    n, d = x.shape
    full = lambda shape: pl.BlockSpec(shape, lambda i: (0, 0))
    return pl.pallas_call(
        _conv_sample_kernel,
        grid=(1,),
        in_specs=[
            full((n, d)),
            full((1, d)),
            pl.BlockSpec((d, d), lambda i: (0, 0)),
            pl.BlockSpec((d, d), lambda i: (0, 1)),
            pl.BlockSpec((d, d), lambda i: (0, 2)),
            full((3, d)),
            full((n, d)),
            full((n, d)),
        ],
        out_specs=[full((n, d)), full((n, d))],
        out_shape=[jax.ShapeDtypeStruct((n, d), BF16), jax.ShapeDtypeStruct((n, d), F32)],
        compiler_params=_cparams(("arbitrary",)),
        name="conv_sample",
    )(x, g, w_in, w_in, w_in, ck, p0, p1)


def kernel(x_prompt, x_sample, cache_k, cache_v, cache_logf, state_conv, page_table,
           norm_mix, norm_ffn, w_ffn_in, w_ffn_out, norm_final,
           attn_w_qkvf, attn_b_f, attn_w_o, conv_w_in, conv_kernel, conv_w_out):
    batch, seq, d = x_prompt.shape
    n_req = x_sample.shape[0]
    assert x_sample.shape[1] == 1 and d == N_HEADS * HEAD_DIM
    assert norm_mix.shape[0] == 2 and cache_k.shape[2] == PAGE_SIZE

    xp = x_prompt.reshape(batch * seq, d)
    xs = x_sample.reshape(n_req, d)
    gfin = norm_final.reshape(1, d)

    wT = attn_w_qkvf[0].T.astype(BF16)
    bf = attn_b_f[0].reshape(N_HEADS, 1)
    g0 = norm_mix[0].reshape(1, d)

    qT, k, kT, vT, vTb, lfT = _proj_prompt(xp, g0, wT, bf, batch=batch, seq=seq)
    cumT, ka = _cumsum(lfT)
    o_p = _attention(k, ka, qT, vTb, cumT, batch=batch, seq=seq)

    qTs, kTs, vTs, lfTs = _proj_sample(xs, g0, wT, bf)
    caches = (jnp.transpose(cache_k[0], (0, 2, 3, 1)),
              jnp.transpose(cache_v[0], (0, 2, 3, 1)),
              jnp.transpose(cache_logf[0], (0, 2, 1)))
    n_pages = page_table.shape[1]
    ffn_steps = (batch * seq // FFN_TM) * (w_ffn_out.shape[1] // FFN_TC)
    capacity = ffn_steps // (n_pages // FFN_DEC_G)
    n_ride = (n_req + 1) // 2
    assert n_ride <= capacity

    def requests(lo, hi):
        return (page_table[lo:hi], qTs[:, lo:hi], kTs[:, lo:hi], vTs[:, lo:hi], lfTs[:, lo:hi], *caches)

    wo = attn_w_o[0].astype(BF16)
    w_in, w_out = w_ffn_in.astype(BF16), w_ffn_out.astype(BF16)
    gf0 = norm_ffn[0].reshape(1, d)
    yp, o_s0 = _mix_ffn(xp, o_p, wo, gf0, w_in, w_out, gfin, layer=0, tm=FFN_TM, final_norm=False,
                        decode=requests(0, n_ride))

    wci = conv_w_in[0].astype(BF16)
    g1 = norm_mix[1].reshape(1, d)
    ck = conv_kernel[0]
    wco = conv_w_out[0].astype(BF16)
    gf1 = norm_ffn[1].reshape(1, d)
    zp, conv_p = _conv_prompt(yp, g1, wci, ck, jnp.zeros((batch, 2, d), F32), batch=batch, seq=seq)
    yp, o_s1 = _mix_ffn(yp, zp, wco, gf1, w_in, w_out, gfin, layer=1, tm=FFN_TM, final_norm=True,
                        decode=requests(n_ride, n_req))

    o_s = jnp.concatenate([o_s0, o_s1], axis=0)
    ys = _mix_ffn(xs, o_s, wo, gf0, w_in, w_out, gfin, layer=0, tm=n_req, final_norm=False)
    zs, cu_s = _conv_sample(ys, g1, wci, ck, state_conv[0, :, 0], state_conv[0, :, 1])
    ys = _mix_ffn(ys, zs, wco, gf1, w_in, w_out, gfin, layer=1, tm=n_req, final_norm=True)

    to_heads = lambda t: jnp.transpose(t.reshape(batch, N_HEADS, HEAD_DIM, seq), (0, 3, 1, 2))[None]
    to_heads_s = lambda t: t.T.reshape(1, n_req, 1, N_HEADS, HEAD_DIM)
    return (
        yp.reshape(batch, seq, d),
        ys.reshape(n_req, 1, d),
        to_heads(kT),
        to_heads(vT),
        jnp.transpose(lfT, (0, 2, 1))[None],
        to_heads_s(kTs),
        to_heads_s(vTs),
        lfTs.T.reshape(1, n_req, 1, N_HEADS),
        conv_p[None],
        jnp.stack([state_conv[0, :, 1], cu_s], axis=1)[None],
    )
```

```python
import functools

import jax
import jax.numpy as jnp
from jax import lax
from jax.experimental import pallas as pl
from jax.experimental.pallas import tpu as pltpu

F32 = jnp.float32
BF16 = jnp.bfloat16

RMS_EPS = 1e-6
NEG_INF = -1e30
N_HEADS = 16
HEAD_DIM = 64
PAGE_SIZE = 128
LANES = 128
VMEM_LIMIT = 56 * 1024 * 1024

PROJ_TM = 512
ATTN_T = 512
ATTN_CH = 64
LOG2E = 1.4426950408889634
FFN_TM = 1024
FFN_TC = 256
CONV_TM = 512
FFN_DEC_G = 8
ATTN_DEC_G = 8


def _cparams(sem):
    return pltpu.CompilerParams(dimension_semantics=sem, vmem_limit_bytes=VMEM_LIMIT)


def _rms(x, g):
    ms = jnp.mean(x * x, axis=-1, keepdims=True)
    return x * lax.rsqrt(ms + RMS_EPS) * g


def _dot(a, b):
    return jnp.dot(a, b, preferred_element_type=F32)


def _nt_dot(a, b):
    return lax.dot_general(a, b, (((1,), (1,)), ((), ())), preferred_element_type=F32)


def _log_sigmoid(x):
    return jnp.minimum(x, 0.0) - jnp.log1p(jnp.exp(-jnp.abs(x)))


def _split3(x):
    hi = x.astype(BF16)
    r1 = x - hi.astype(F32)
    mid = r1.astype(BF16)
    lo = (r1 - mid.astype(F32)).astype(BF16)
    return hi, mid, lo


def _dot3(x, w):
    hi, mid, lo = _split3(x)
    return _dot(hi, w) + _dot(mid, w) + _dot(lo, w)


def _nt_dot3(w, x):
    hi, mid, lo = _split3(x)
    return _nt_dot(w, hi) + _nt_dot(w, mid) + _nt_dot(w, lo)


def _qkvf_specs(d, index):
    return [
        pl.BlockSpec((d, d), index(0)),
        pl.BlockSpec((d, d), index(1)),
        pl.BlockSpec((d, d), index(2)),
        pl.BlockSpec((N_HEADS, d), index(3 * d // N_HEADS)),
    ]


def _proj_prompt_kernel(x_ref, g_ref, wqT_ref, wkT_ref, wvT_ref, wfT_ref, bf_ref,
                        qT_ref, k_ref, kT_ref, vT_ref, vTb_ref, lfT_ref, *, scale):
    h = _rms(x_ref[...], g_ref[...]).astype(BF16)
    qT_ref[...] = (_nt_dot(wqT_ref[...], h) * scale).astype(BF16)
    kT = _nt_dot(wkT_ref[...], h)
    kT_ref[...] = kT
    k = kT.T.astype(BF16)
    for hp in range(N_HEADS // 2):
        k_ref[hp] = k[:, hp * LANES:(hp + 1) * LANES]
    vT = _nt_dot(wvT_ref[...], h)
    vT_ref[...] = vT
    vTb_ref[...] = vT.astype(BF16)
    lfT_ref[...] = _log_sigmoid(_nt_dot(wfT_ref[...], h) + bf_ref[...])


def _proj_prompt(x, g, wT, bf, *, batch, seq):
    m, d = x.shape
    tm = PROJ_TM
    nb = seq // tm
    n_pairs = N_HEADS // 2
    const = lambda i: (0, 0)
    tmaj = lambda i: (i // nb, 0, i % nb)
    return pl.pallas_call(
        functools.partial(_proj_prompt_kernel, scale=HEAD_DIM ** -0.5 * LOG2E),
        grid=(m // tm,),
        in_specs=[
            pl.BlockSpec((tm, d), lambda i: (i, 0)),
            pl.BlockSpec((1, d), const),
            *_qkvf_specs(d, lambda k: (lambda i: (k, 0))),
            pl.BlockSpec((N_HEADS, 1), const),
        ],
        out_specs=[
            pl.BlockSpec((None, d, tm), tmaj),
            pl.BlockSpec((n_pairs, tm, LANES), lambda i: (0, i, 0)),
            pl.BlockSpec((None, d, tm), tmaj),
            pl.BlockSpec((None, d, tm), tmaj),
            pl.BlockSpec((None, d, tm), tmaj),
            pl.BlockSpec((None, N_HEADS, tm), tmaj),
        ],
        out_shape=[
            jax.ShapeDtypeStruct((batch, d, seq), BF16),
            jax.ShapeDtypeStruct((n_pairs, m, LANES), BF16),
            jax.ShapeDtypeStruct((batch, d, seq), F32),
            jax.ShapeDtypeStruct((batch, d, seq), F32),
            jax.ShapeDtypeStruct((batch, d, seq), BF16),
            jax.ShapeDtypeStruct((batch, N_HEADS, seq), F32),
        ],
        compiler_params=_cparams(("arbitrary",)),
        name="proj_prompt",
    )(x, g, wT, wT, wT, wT, bf)


def _proj_sample_kernel(x_ref, g_ref, wqT_ref, wkT_ref, wvT_ref, wfT_ref, bf_ref,
                        qT_ref, kT_ref, vT_ref, lfT_ref, *, scale):
    h = _rms(x_ref[...], g_ref[...]).astype(BF16)
    qT_ref[...] = _nt_dot(wqT_ref[...], h) * scale
    kT_ref[...] = _nt_dot(wkT_ref[...], h)
    vT_ref[...] = _nt_dot(wvT_ref[...], h)
    lfT_ref[...] = _log_sigmoid(_nt_dot(wfT_ref[...], h) + bf_ref[...])


def _proj_sample(x, g, wT, bf):
    n, d = x.shape
    full = lambda shape: pl.BlockSpec(shape, lambda i: (0, 0))
    return pl.pallas_call(
        functools.partial(_proj_sample_kernel, scale=HEAD_DIM ** -0.5),
        grid=(1,),
        in_specs=[
            full((n, d)),
            full((1, d)),
            *_qkvf_specs(d, lambda k: (lambda i: (k, 0))),
            full((N_HEADS, 1)),
        ],
        out_specs=[full((d, n)), full((d, n)), full((d, n)), full((N_HEADS, n))],
        out_shape=[
            jax.ShapeDtypeStruct((d, n), F32),
            jax.ShapeDtypeStruct((d, n), F32),
            jax.ShapeDtypeStruct((d, n), F32),
            jax.ShapeDtypeStruct((N_HEADS, n), F32),
        ],
        compiler_params=_cparams(("arbitrary",)),
        name="proj_sample",
    )(x, g, wT, wT, wT, wT, bf)


def _cumsum_kernel(lfT_ref, cumT_ref, ka_ref, *, seq):
    blk = LANES
    n_pairs = N_HEADS // 2
    r = lax.broadcasted_iota(jnp.int32, (blk, blk), 0)
    c = lax.broadcasted_iota(jnp.int32, (blk, blk), 1)
    upper = (r <= c).astype(BF16)
    eye = (r == c).astype(BF16)
    head = lax.broadcasted_iota(jnp.int32, (N_HEADS, n_pairs * LANES), 0)
    col = lax.broadcasted_iota(jnp.int32, (N_HEADS, n_pairs * LANES), 1)
    target = (head >> 1) * LANES + 3 + 3 * (head & 1)
    place = [(col == target + part).astype(BF16) for part in range(3)]
    ones_lanes = ((lax.broadcasted_iota(jnp.int32, (1, n_pairs * LANES), 1) & (LANES - 1)) < 3).astype(F32)
    carry = jnp.zeros((N_HEADS, 1), F32)
    for i in range(seq // blk):
        sl = slice(i * blk, (i + 1) * blk)
        cb = _dot3(lfT_ref[:, sl], upper) + carry
        cumT_ref[:, sl] = cb
        carry = cb[:, blk - 1:blk]
        cbt = _nt_dot3(eye, cb)
        hi, mid, lo = _split3(-(cbt * LOG2E))
        ka = (_dot(hi, place[0]) + _dot(mid, place[1]) + _dot(lo, place[2]) + ones_lanes).astype(BF16)
        for hp in range(n_pairs):
            ka_ref[hp, sl, :] = ka[:, hp * LANES:(hp + 1) * LANES]


def _cumsum(lfT):
    batch, nh, seq = lfT.shape
    n_pairs = nh // 2
    return pl.pallas_call(
        functools.partial(_cumsum_kernel, seq=seq),
        grid=(batch,),
        in_specs=[pl.BlockSpec((None, nh, seq), lambda b: (b, 0, 0))],
        out_specs=[
            pl.BlockSpec((None, nh, seq), lambda b: (b, 0, 0)),
            pl.BlockSpec((n_pairs, seq, LANES), lambda b: (0, b, 0)),
        ],
        out_shape=[
            jax.ShapeDtypeStruct((batch, nh, seq), F32),
            jax.ShapeDtypeStruct((n_pairs, batch * seq, LANES), BF16),
        ],
        compiler_params=_cparams(("arbitrary",)),
        name="cumsum",
    )(lfT)


def _attn_kernel(qi_ref, kj_ref, pt_ref, k_ref, ka_ref, qT_ref, vT_ref, cumT_ref, *rest, t, dec):
    n_dec, group, spr = dec
    qTs_ref, kTs_ref, vTs_ref, lfTs_ref = rest[0:4]
    dk_refs = rest[4:4 + group]
    dv_refs = rest[4 + group:4 + 2 * group]
    dlf_refs = rest[4 + 2 * group:4 + 3 * group]
    o_ref, o_dec_ref = rest[4 + 3 * group:6 + 3 * group]
    qop_ref, m_ref, l_ref, al_ref, acc_ref, s_ref, p_ref, oT_ref = rest[6 + 3 * group:14 + 3 * group]
    dstate = rest[14 + 3 * group:]
    gstep = pl.program_id(0) * pl.num_programs(1) + pl.program_id(1)
    first = (pl.program_id(0) == 0) & (pl.program_id(1) == 0)
    i = qi_ref[pl.program_id(1)]
    j = kj_ref[pl.program_id(1)]
    n_pairs = N_HEADS // 2
    ch = ATTN_CH
    sub = 8

    @pl.when(first)
    def _():
        qop_ref[...] = jnp.zeros(qop_ref.shape, BF16)

    @pl.when(j == 0)
    def _():
        m_ref[...] = jnp.full(m_ref.shape, NEG_INF, F32)
        l_ref[...] = jnp.zeros(l_ref.shape, F32)
        acc_ref[...] = jnp.zeros(acc_ref.shape, F32)
        parts = [x.astype(F32) for x in _split3(cumT_ref[...] * LOG2E)]
        row16 = lax.broadcasted_iota(jnp.int32, (16, t), 0)
        for h in range(N_HEADS):
            e = h % 2
            own = slice(e * HEAD_DIM, (e + 1) * HEAD_DIM)
            qop_ref[h, own, :] = qT_ref[h // 2, own, :]
            hi, mid, lo = (x[h:h + 1, :] for x in parts)
            ones = jnp.where((row16 >= 3 + 3 * e) & (row16 < 6 + 3 * e), 1.0, 0.0)
            ext = jnp.where(row16 == 0, hi, jnp.where(row16 == 1, mid, jnp.where(row16 == 2, lo, ones)))
            qop_ref[h, LANES:LANES + 16, :] = ext.astype(BF16)

    @pl.when(gstep % spr == 0)
    def _():
        _decode_init(gstep // spr, n_dec, qTs_ref, kTs_ref, vTs_ref, lfTs_ref, dstate)

    def step(diagonal):
        _decode_scores(dk_refs, dstate)
        _decode_probs(dlf_refs, dstate)
        _decode_values(dv_refs, dstate)
        if diagonal:
            row_minus_col = (lax.broadcasted_iota(jnp.int32, (ch, LANES), 0)
                             - lax.broadcasted_iota(jnp.int32, (ch, LANES), 1))
        n_ch, n_u = t // ch, t // LANES
        ones_rows = jnp.ones((16, t), BF16)

        def above(r, u):
            return diagonal and r * ch > u * LANES + LANES - 1

        def crosses(r, u):
            return diagonal and not above(r, u) and r * ch + ch - 1 > u * LANES

        def scores(hp, slot):
            kx = jnp.concatenate([k_ref[hp], ka_ref[hp]], axis=1)
            for e in range(2):
                s_ref[slot, e] = _dot(kx, qop_ref[2 * hp + e])

        def softmax(hp, slot):
            for e in range(2):
                h = 2 * hp + e
                for u in range(n_u):
                    lsl = slice(u * LANES, (u + 1) * LANES)
                    mx = None
                    for r in range(n_ch):
                        if above(r, u):
                            continue
                        rows = slice(r * ch, (r + 1) * ch)
                        blk = s_ref[slot, e, rows, lsl]
                        if crosses(r, u):
                            blk = jnp.where(row_minus_col <= (u * LANES - r * ch), blk, NEG_INF)
                            s_ref[slot, e, rows, lsl] = blk
                        part = jnp.max(blk.reshape(ch // sub, sub, LANES), axis=0)
                        mx = part if mx is None else jnp.maximum(mx, part)
                    m_old = m_ref[h, :, lsl]
                    m_new = jnp.maximum(m_old, jnp.max(mx, axis=0, keepdims=True))
                    al_ref[slot, e, :, lsl] = jnp.exp2(m_old - m_new)
                    m_ref[h, :, lsl] = m_new
                    for r in range(n_ch):
                        rows = slice(r * ch, (r + 1) * ch)
                        if above(r, u):
                            p_ref[slot, e, rows, lsl] = jnp.zeros((ch, LANES), BF16)
                            continue
                        blk = s_ref[slot, e, rows, lsl].reshape(ch // sub, sub, LANES)
                        p = jnp.exp2(blk - m_new[None])
                        p_ref[slot, e, rows, lsl] = p.reshape(ch, LANES).astype(BF16)

        def values(hp, slot):
            for e in range(2):
                h = 2 * hp + e
                vx = jnp.concatenate([vT_ref[h], ones_rows], axis=0)
                c = _dot(vx, p_ref[slot, e])
                al = al_ref[slot, e]
                acc = acc_ref[h].reshape(HEAD_DIM // sub, sub, t) * al[None]
                acc_ref[h] = acc.reshape(HEAD_DIM, t) + c[0:HEAD_DIM]
                l_ref[h] = al * l_ref[h] + c[HEAD_DIM:HEAD_DIM + sub]

        scores(0, 0)
        for k in range(n_pairs):
            if k + 1 < n_pairs:
                scores(k + 1, (k + 1) % 2)
            softmax(k, k % 2)
            values(k, k % 2)

    @pl.when(j < i)
    def _():
        step(False)

    @pl.when(j == i)
    def _():
        step(True)
        for h in range(N_HEADS):
            o = acc_ref[h].reshape(HEAD_DIM // sub, sub, t) / l_ref[h][None]
            oT_ref[h * HEAD_DIM:(h + 1) * HEAD_DIM, :] = o.reshape(HEAD_DIM, t)
        o_ref[...] = oT_ref[...].T.astype(BF16)

    @pl.when(gstep % spr == spr - 1)
    def _():
        _decode_finish(o_dec_ref, dstate)


def _attention(k, ka, qT, vTb, cumT, decode, *, batch, seq):
    n_pairs, m, _ = k.shape
    d = N_HEADS * HEAD_DIM
    t = ATTN_T
    nq = seq // t
    pairs = [(i, j) for i in range(nq) for j in range(i + 1)]
    qi = jnp.array([i for i, _ in pairs], jnp.int32)
    kj = jnp.array([j for _, j in pairs], jnp.int32)
    page_table, qTs, kTs, vTs, lfTs, kT_cache, vT_cache, lfT_cache = decode
    n_dec, n_pages = page_table.shape
    group = ATTN_DEC_G
    spr = n_pages // group
    n_steps = len(pairs)
    assert batch * n_steps == n_dec * spr, "every grid step must carry one page group"

    def page_of(g, b, s, qi, kj, pt):
        step = b * n_steps + s
        return pt[step // spr, n_pages - 1 - ((step % spr) * group + g)]

    kvrow = lambda b, s, qi, kj, pt: (0, b * nq + kj[s], 0)
    grid_spec = pltpu.PrefetchScalarGridSpec(
        num_scalar_prefetch=3,
        grid=(batch, n_steps),
        in_specs=[
            pl.BlockSpec((n_pairs, t, LANES), kvrow),
            pl.BlockSpec((n_pairs, t, LANES), kvrow),
            pl.BlockSpec((None, n_pairs, LANES, t), lambda b, s, qi, kj, pt: (b, 0, 0, qi[s])),
            pl.BlockSpec((None, N_HEADS, HEAD_DIM, t), lambda b, s, qi, kj, pt: (b, 0, 0, kj[s])),
            pl.BlockSpec((None, N_HEADS, t), lambda b, s, qi, kj, pt: (b, 0, qi[s])),
        ] + _decode_specs(n_dec, group, page_of),
        out_specs=[
            pl.BlockSpec((t, d), lambda b, s, qi, kj, pt: (b * nq + qi[s], 0)),
            pl.BlockSpec((None, 1, d), lambda b, s, qi, kj, pt: ((b * n_steps + s) // spr, 0, 0)),
        ],
        scratch_shapes=[
            pltpu.VMEM((N_HEADS, 2 * LANES, t), BF16),
            pltpu.VMEM((N_HEADS, 8, t), F32),
            pltpu.VMEM((N_HEADS, 8, t), F32),
            pltpu.VMEM((2, 2, 8, t), F32),
            pltpu.VMEM((N_HEADS, HEAD_DIM, t), F32),
            pltpu.VMEM((2, 2, t, t), F32),
            pltpu.VMEM((2, 2, t, t), BF16),
            pltpu.VMEM((d, t), F32),
        ] + _decode_scratch(group),
    )
    o, o_dec = pl.pallas_call(
        functools.partial(_attn_kernel, t=t, dec=(n_dec, group, spr)),
        grid_spec=grid_spec,
        out_shape=[jax.ShapeDtypeStruct((m, d), BF16), jax.ShapeDtypeStruct((n_dec, 1, d), BF16)],
        compiler_params=_cparams(("arbitrary", "arbitrary")),
        name="attention",
    )(qi, kj, page_table, k, ka, qT.reshape(batch, n_pairs, LANES, seq),
      vTb.reshape(batch, N_HEADS, HEAD_DIM, seq), cumT, qTs, kTs, vTs, lfTs,
      *([kT_cache] * group), *([vT_cache] * group), *([lfT_cache] * group))
    return o, o_dec.reshape(n_dec, d)


def _decode_init(b, n_req, qT_ref, kTs_ref, vTs_ref, lfTs_ref, state):
    qb_ref, acc_ref, m_ref, l_ref, carry_ref = state[:5]
    sel = lax.broadcasted_iota(jnp.int32, (1, n_req), 1) == b
    pick = lambda ref: jnp.sum(jnp.where(sel, ref[...], 0.0), axis=1, keepdims=True)
    qcol = pick(qT_ref)
    kcol = pick(kTs_ref)
    vcol = pick(vTs_ref)
    qk = qcol * kcol
    lane0 = lax.broadcasted_iota(jnp.int32, (1, LANES), 1) == 0
    for h in range(N_HEADS):
        hs = slice(h * HEAD_DIM, (h + 1) * HEAD_DIM)
        qb_ref[h] = jnp.broadcast_to(qcol[hs], (HEAD_DIM, LANES))
        acc_ref[h] = jnp.where(lane0, vcol[hs], 0.0)
        m_ref[h:h + 1, :] = jnp.sum(qk[hs], axis=0, keepdims=True)
    l_ref[...] = jnp.ones(l_ref.shape, F32)
    carry_ref[...] = pick(lfTs_ref)


def _decode_scores(k_refs, state, heads=range(N_HEADS)):
    qb_ref, s_ref = state[0], state[5]
    for h in heads:
        qb = qb_ref[h]
        for g, k_ref in enumerate(k_refs):
            s_ref[g, h:h + 1, :] = jnp.sum(k_ref[h] * qb, axis=0, keepdims=True)


def _decode_probs(lf_refs, state):
    _, _, m_ref, l_ref, carry_ref, s_ref, p_ref, ab_ref = state
    r = lax.broadcasted_iota(jnp.int32, (PAGE_SIZE, PAGE_SIZE), 0)
    c = lax.broadcasted_iota(jnp.int32, (PAGE_SIZE, PAGE_SIZE), 1)
    after = (r > c).astype(BF16)
    carry = carry_ref[...]
    logits = []
    for g, lf_ref in enumerate(lf_refs):
        lfT = lf_ref[...]
        logits.append(s_ref[g] + (_dot3(lfT, after) + carry))
        carry = carry + jnp.sum(lfT, axis=1, keepdims=True)
    carry_ref[...] = carry
    m_prev = m_ref[...]
    m_next = jnp.maximum(m_prev, jnp.max(functools.reduce(jnp.maximum, logits), axis=1, keepdims=True))
    alpha = jnp.exp(m_prev - m_next)
    psum = None
    for g in range(len(lf_refs)):
        p = jnp.exp(logits[g] - m_next)
        p_ref[g] = p
        psum = p if psum is None else psum + p
    l_ref[...] = alpha * l_ref[...] + jnp.sum(psum, axis=1, keepdims=True)
    m_ref[...] = m_next
    ab_ref[...] = jnp.broadcast_to(alpha, ab_ref.shape)


def _decode_values(v_refs, state, heads=range(N_HEADS)):
    acc_ref, p_ref, ab_ref = state[1], state[6], state[7]
    for h in heads:
        acc = ab_ref[h:h + 1, :] * acc_ref[h]
        for g, v_ref in enumerate(v_refs):
            acc = acc + v_ref[h] * p_ref[g, h:h + 1, :]
        acc_ref[h] = acc


def _decode_finish(o_ref, state):
    acc_ref, l_ref = state[1], state[3]
    for h in range(N_HEADS):
        acc_ref[h] = acc_ref[h] / l_ref[h:h + 1, :]
    ones = jnp.ones((8, LANES), BF16)
    o8 = _nt_dot3(ones, acc_ref[...].reshape(N_HEADS * HEAD_DIM, LANES))
    o_ref[...] = o8[0:1, :].astype(BF16)


def _decode_scratch(group):
    return [
        pltpu.VMEM((N_HEADS, HEAD_DIM, LANES), F32),
        pltpu.VMEM((N_HEADS, HEAD_DIM, LANES), F32),
        pltpu.VMEM((N_HEADS, 1), F32),
        pltpu.VMEM((N_HEADS, 1), F32),
        pltpu.VMEM((N_HEADS, 1), F32),
        pltpu.VMEM((group, N_HEADS, PAGE_SIZE), F32),
        pltpu.VMEM((group, N_HEADS, PAGE_SIZE), F32),
        pltpu.VMEM((N_HEADS, LANES), F32),
    ]


def _decode_specs(n_req, group, page_of):
    d = N_HEADS * HEAD_DIM
    small = lambda *a: (0, 0)
    kv_block = (None, N_HEADS, HEAD_DIM, PAGE_SIZE)
    kv_map = lambda g: (lambda *a: (page_of(g, *a), 0, 0, 0))
    lf_map = lambda g: (lambda *a: (page_of(g, *a), 0, 0))
    specs = [pl.BlockSpec((d, n_req), small)] * 3 + [pl.BlockSpec((N_HEADS, n_req), small)]
    specs += [pl.BlockSpec(kv_block, kv_map(g)) for g in range(group)]
    specs += [pl.BlockSpec(kv_block, kv_map(g)) for g in range(group)]
    specs += [pl.BlockSpec((None, N_HEADS, PAGE_SIZE), lf_map(g)) for g in range(group)]
    return specs


def _ffn_kernel(*refs, n_chunks, final_norm, dec):
    if dec is None:
        x_ref, z_ref, wmo_ref, g_ref, wg_ref, wu_ref, wo_ref, gf_ref, out_ref, acc_ref, h_ref = refs
    else:
        n_req, group, spr = dec
        x_ref, z_ref, wmo_ref, g_ref, wg_ref, wu_ref, wo_ref, gf_ref = refs[1:9]
        qT_ref, kTs_ref, vTs_ref, lfTs_ref = refs[9:13]
        k_refs = refs[13:13 + group]
        v_refs = refs[13 + group:13 + 2 * group]
        lf_refs = refs[13 + 2 * group:13 + 3 * group]
        out_ref, o_dec_ref, acc_ref, h_ref = refs[13 + 3 * group:17 + 3 * group]
        state = refs[17 + 3 * group:]
        step = pl.program_id(0) * n_chunks + pl.program_id(1)
    c = pl.program_id(1)

    @pl.when(c == 0)
    def _():
        y1 = x_ref[...] + _dot(z_ref[...], wmo_ref[...])
        acc_ref[...] = y1
        h_ref[...] = _rms(y1, g_ref[...]).astype(BF16)

    if dec is not None:
        active = step < n_req * spr

        @pl.when((step % spr == 0) & active)
        def _():
            _decode_init(step // spr, n_req, qT_ref, kTs_ref, vTs_ref, lfTs_ref, state)

    def chunk(with_pages):
        if with_pages:
            _decode_scores(k_refs, state)
            _decode_probs(lf_refs, state)
            _decode_values(v_refs, state)
        h = h_ref[...]
        gate = _dot(h, wg_ref[...])
        up = _dot(h, wu_ref[...])
        a = (gate * jax.nn.sigmoid(gate) * up).astype(BF16)
        acc_ref[...] += _dot(a, wo_ref[...])

    if dec is None:
        chunk(False)
    else:
        pl.when(active)(lambda: chunk(True))
        pl.when(jnp.logical_not(active))(lambda: chunk(False))

    @pl.when(c == n_chunks - 1)
    def _():
        y = acc_ref[...]
        out_ref[...] = _rms(y, gf_ref[...]) if final_norm else y

    if dec is not None:
        @pl.when((step % spr == spr - 1) & active)
        def _():
            _decode_finish(o_dec_ref, state)


def _mix_ffn(x, z, wmo, g, w_in, w_out, g_final, *, layer, tm, final_norm, decode=None):
    m, d = x.shape
    d_ff = w_out.shape[1]
    tc = FFN_TC
    n_chunks = d_ff // tc
    row = lambda i, c, *pt: (i, 0)
    const = lambda i, c, *pt: (0, 0)
    in_specs = [
        pl.BlockSpec((tm, d), row),
        pl.BlockSpec((tm, d), row),
        pl.BlockSpec((d, d), const),
        pl.BlockSpec((1, d), const),
        pl.BlockSpec((None, d, tc), lambda i, c, *pt: (layer, 0, c)),
        pl.BlockSpec((None, d, tc), lambda i, c, *pt: (layer, 0, n_chunks + c)),
        pl.BlockSpec((None, tc, d), lambda i, c, *pt: (layer, c, 0)),
        pl.BlockSpec((1, d), const),
    ]
    scratch = [pltpu.VMEM((tm, d), F32), pltpu.VMEM((tm, d), BF16)]
    operands = (x, z, wmo, g, w_in, w_in, w_out, g_final)
    grid = (m // tm, n_chunks)
    y_spec = pl.BlockSpec((tm, d), row)
    y_shape = jax.ShapeDtypeStruct((m, d), F32)
    if decode is None:
        return pl.pallas_call(
            functools.partial(_ffn_kernel, n_chunks=n_chunks, final_norm=final_norm, dec=None),
            grid=grid, in_specs=in_specs, out_specs=y_spec, out_shape=y_shape,
            scratch_shapes=scratch,
            compiler_params=_cparams(("arbitrary", "arbitrary")),
            name="mix_ffn",
        )(*operands)

    page_table, qT, kTs, vTs, lfTs, kT_cache, vT_cache, lfT_cache = decode
    n_req, n_pages = page_table.shape
    group = FFN_DEC_G
    spr = n_pages // group
    assert grid[0] * grid[1] >= n_req * spr, "not enough grid steps for these requests"

    def request_of(i, c):
        step = i * n_chunks + c
        valid = step < n_req * spr
        return jnp.where(valid, step // spr, n_req - 1), jnp.where(valid, step % spr, spr - 1)

    def page_of(g, i, c, pt):
        req, part = request_of(i, c)
        return pt[req, n_pages - 1 - (part * group + g)]

    grid_spec = pltpu.PrefetchScalarGridSpec(
        num_scalar_prefetch=1,
        grid=grid,
        in_specs=in_specs + _decode_specs(n_req, group, page_of),
        out_specs=[y_spec, pl.BlockSpec((None, 1, d), lambda i, c, pt: (request_of(i, c)[0], 0, 0))],
        scratch_shapes=scratch + _decode_scratch(group),
    )
    y, o_dec = pl.pallas_call(
        functools.partial(_ffn_kernel, n_chunks=n_chunks, final_norm=final_norm, dec=(n_req, group, spr)),
        grid_spec=grid_spec,
        out_shape=[y_shape, jax.ShapeDtypeStruct((n_req, 1, d), BF16)],
        compiler_params=_cparams(("arbitrary", "arbitrary")),
        name="mix_ffn_decode",
    )(page_table, *operands, qT, kTs, vTs, lfTs,
      *([kT_cache] * group), *([vT_cache] * group), *([lfT_cache] * group))
    return y, o_dec.reshape(n_req, d)


def _conv_prompt_kernel(x_ref, g_ref, wb_ref, wc_ref, wu_ref, ck_ref, prev_ref, z_ref, st_ref,
                        buf_ref, *, tm, blocks_per_seq):
    i = pl.program_id(0)
    h = _rms(x_ref[...], g_ref[...]).astype(BF16)
    gate = _dot(h, wb_ref[...])
    cu = _dot(h, wc_ref[...]) * _dot(h, wu_ref[...])

    @pl.when(i % blocks_per_seq == 0)
    def _():
        buf_ref[6:8, :] = prev_ref[...]

    buf_ref[8:8 + tm, :] = cu
    y = ck_ref[0:1, :] * buf_ref[6:6 + tm, :] + ck_ref[1:2, :] * buf_ref[7:7 + tm, :]
    y = y + ck_ref[2:3, :] * cu
    z_ref[...] = (gate * y).astype(BF16)
    tail = cu[tm - 2:tm, :]
    st_ref[...] = tail
    buf_ref[6:8, :] = tail


def _conv_prompt(x, g, w_in, ck, prev, *, batch, seq):
    m, d = x.shape
    tm = CONV_TM
    bps = seq // tm
    const = lambda i: (0, 0)
    return pl.pallas_call(
        functools.partial(_conv_prompt_kernel, tm=tm, blocks_per_seq=bps),
        grid=(m // tm,),
        in_specs=[
            pl.BlockSpec((tm, d), lambda i: (i, 0)),
            pl.BlockSpec((1, d), const),
            pl.BlockSpec((d, d), lambda i: (0, 0)),
            pl.BlockSpec((d, d), lambda i: (0, 1)),
            pl.BlockSpec((d, d), lambda i: (0, 2)),
            pl.BlockSpec((3, d), const),
            pl.BlockSpec((None, 2, d), lambda i: (i // bps, 0, 0)),
        ],
        out_specs=[
            pl.BlockSpec((tm, d), lambda i: (i, 0)),
            pl.BlockSpec((None, 2, d), lambda i: (i // bps, 0, 0)),
        ],
        out_shape=[
            jax.ShapeDtypeStruct((m, d), BF16),
            jax.ShapeDtypeStruct((batch, 2, d), F32),
        ],
        scratch_shapes=[pltpu.VMEM((tm + 8, d), F32)],
        compiler_params=_cparams(("arbitrary",)),
        name="conv_prompt",
    )(x, g, w_in, w_in, w_in, ck, prev)


def _conv_sample_kernel(x_ref, g_ref, wb_ref, wc_ref, wu_ref, ck_ref, p0_ref, p1_ref, z_ref, cu_ref):
    h = _rms(x_ref[...], g_ref[...]).astype(BF16)
    gate = _dot(h, wb_ref[...])
    cu = _dot(h, wc_ref[...]) * _dot(h, wu_ref[...])
    y = ck_ref[0:1, :] * p0_ref[...] + ck_ref[1:2, :] * p1_ref[...]
    y = y + ck_ref[2:3, :] * cu
    z_ref[...] = (gate * y).astype(BF16)
    cu_ref[...] = cu


def _conv_sample(x, g, w_in, ck, p0, p1):
    n, d = x.shape
    full = lambda shape: pl.BlockSpec(shape, lambda i: (0, 0))
    return pl.pallas_call(
        _conv_sample_kernel,
        grid=(1,),
        in_specs=[
            full((n, d)),
            full((1, d)),
            pl.BlockSpec((d, d), lambda i: (0, 0)),
            pl.BlockSpec((d, d), lambda i: (0, 1)),
            pl.BlockSpec((d, d), lambda i: (0, 2)),
            full((3, d)),
            full((n, d)),
            full((n, d)),
        ],
        out_specs=[full((n, d)), full((n, d))],
        out_shape=[jax.ShapeDtypeStruct((n, d), BF16), jax.ShapeDtypeStruct((n, d), F32)],
        compiler_params=_cparams(("arbitrary",)),
        name="conv_sample",
    )(x, g, w_in, w_in, w_in, ck, p0, p1)


def kernel(x_prompt, x_sample, cache_k, cache_v, cache_logf, state_conv, page_table,
           norm_mix, norm_ffn, w_ffn_in, w_ffn_out, norm_final,
           attn_w_qkvf, attn_b_f, attn_w_o, conv_w_in, conv_kernel, conv_w_out):
    batch, seq, d = x_prompt.shape
    n_req = x_sample.shape[0]
    assert x_sample.shape[1] == 1 and d == N_HEADS * HEAD_DIM
    assert norm_mix.shape[0] == 2 and cache_k.shape[2] == PAGE_SIZE

    xp = x_prompt.reshape(batch * seq, d)
    xs = x_sample.reshape(n_req, d)
    gfin = norm_final.reshape(1, d)

    wT = attn_w_qkvf[0].T.astype(BF16)
    bf = attn_b_f[0].reshape(N_HEADS, 1)
    g0 = norm_mix[0].reshape(1, d)

    qT, k, kT, vT, vTb, lfT = _proj_prompt(xp, g0, wT, bf, batch=batch, seq=seq)
    cumT, ka = _cumsum(lfT)

    qTs, kTs, vTs, lfTs = _proj_sample(xs, g0, wT, bf)
    caches = (jnp.transpose(cache_k[0], (0, 2, 3, 1)),
              jnp.transpose(cache_v[0], (0, 2, 3, 1)),
              jnp.transpose(cache_logf[0], (0, 2, 1)))
    n_pages = page_table.shape[1]
    nq = seq // ATTN_T
    n_att = batch * (nq * (nq + 1) // 2) // (n_pages // ATTN_DEC_G)
    ffn_steps = (batch * seq // FFN_TM) * (w_ffn_out.shape[1] // FFN_TC)
    capacity = ffn_steps // (n_pages // FFN_DEC_G)
    n_ride = (n_req - n_att + 1) // 2
    assert 0 < n_att < n_req and 0 < n_ride <= capacity and n_att + n_ride < n_req

    def requests(lo, hi):
        return (page_table[lo:hi], qTs[:, lo:hi], kTs[:, lo:hi], vTs[:, lo:hi], lfTs[:, lo:hi], *caches)

    o_p, o_s0 = _attention(k, ka, qT, vTb, cumT, requests(0, n_att), batch=batch, seq=seq)

    wo = attn_w_o[0].astype(BF16)
    w_in, w_out = w_ffn_in.astype(BF16), w_ffn_out.astype(BF16)
    gf0 = norm_ffn[0].reshape(1, d)
    yp, o_s1 = _mix_ffn(xp, o_p, wo, gf0, w_in, w_out, gfin, layer=0, tm=FFN_TM, final_norm=False,
                        decode=requests(n_att, n_att + n_ride))

    wci = conv_w_in[0].astype(BF16)
    g1 = norm_mix[1].reshape(1, d)
    ck = conv_kernel[0]
    wco = conv_w_out[0].astype(BF16)
    gf1 = norm_ffn[1].reshape(1, d)
    zp, conv_p = _conv_prompt(yp, g1, wci, ck, jnp.zeros((batch, 2, d), F32), batch=batch, seq=seq)
    yp, o_s2 = _mix_ffn(yp, zp, wco, gf1, w_in, w_out, gfin, layer=1, tm=FFN_TM, final_norm=True,
                        decode=requests(n_att + n_ride, n_req))

    o_s = jnp.concatenate([o_s0, o_s1, o_s2], axis=0)
    ys = _mix_ffn(xs, o_s, wo, gf0, w_in, w_out, gfin, layer=0, tm=n_req, final_norm=False)
    zs, cu_s = _conv_sample(ys, g1, wci, ck, state_conv[0, :, 0], state_conv[0, :, 1])
    ys = _mix_ffn(ys, zs, wco, gf1, w_in, w_out, gfin, layer=1, tm=n_req, final_norm=True)

    to_heads = lambda t: jnp.transpose(t.reshape(batch, N_HEADS, HEAD_DIM, seq), (0, 3, 1, 2))[None]
    to_heads_s = lambda t: t.T.reshape(1, n_req, 1, N_HEADS, HEAD_DIM)
    return (
        yp.reshape(batch, seq, d),
        ys.reshape(n_req, 1, d),
        to_heads(kT),
        to_heads(vT),
        jnp.transpose(lfT, (0, 2, 1))[None],
        to_heads_s(kTs),
        to_heads_s(vTs),
        lfTs.T.reshape(1, n_req, 1, N_HEADS),
        conv_p[None],
        jnp.stack([state_conv[0, :, 1], cu_s], axis=1)[None],
    )
```

```python
import functools

import jax
import jax.numpy as jnp
from jax import lax
from jax.experimental import pallas as pl
from jax.experimental.pallas import tpu as pltpu

F32 = jnp.float32
BF16 = jnp.bfloat16

RMS_EPS = 1e-6
NEG_INF = -1e30
N_HEADS = 16
HEAD_DIM = 64
PAGE_SIZE = 128
LANES = 128
VMEM_LIMIT = 56 * 1024 * 1024

PROJ_TM = 512
ATTN_T = 512
ATTN_CH = 64
LOG2E = 1.4426950408889634
FFN_TM = 1024
FFN_TC = 256
CONV_TM = 1024
FFN_DEC_G = 8


def _cparams(sem):
    return pltpu.CompilerParams(dimension_semantics=sem, vmem_limit_bytes=VMEM_LIMIT)


def _rms(x, g):
    ms = jnp.mean(x * x, axis=-1, keepdims=True)
    return x * lax.rsqrt(ms + RMS_EPS) * g


def _dot(a, b):
    return jnp.dot(a, b, preferred_element_type=F32)


def _nt_dot(a, b):
    return lax.dot_general(a, b, (((1,), (1,)), ((), ())), preferred_element_type=F32)


def _log_sigmoid(x):
    return jnp.minimum(x, 0.0) - jnp.log1p(jnp.exp(-jnp.abs(x)))


def _split3(x):
    hi = x.astype(BF16)
    r1 = x - hi.astype(F32)
    mid = r1.astype(BF16)
    lo = (r1 - mid.astype(F32)).astype(BF16)
    return hi, mid, lo


def _dot3(x, w):
    hi, mid, lo = _split3(x)
    return _dot(hi, w) + _dot(mid, w) + _dot(lo, w)


def _nt_dot3(w, x):
    hi, mid, lo = _split3(x)
    return _nt_dot(w, hi) + _nt_dot(w, mid) + _nt_dot(w, lo)


def _qkvf_specs(d, index):
    return [
        pl.BlockSpec((d, d), index(0)),
        pl.BlockSpec((d, d), index(1)),
        pl.BlockSpec((d, d), index(2)),
        pl.BlockSpec((N_HEADS, d), index(3 * d // N_HEADS)),
    ]


def _proj_prompt_kernel(x_ref, g_ref, wqT_ref, wkT_ref, wvT_ref, wfT_ref, bf_ref,
                        qT_ref, k_ref, kT_ref, vT_ref, vTb_ref, lfT_ref, *, scale):
    h = _rms(x_ref[...], g_ref[...]).astype(BF16)
    qT_ref[...] = (_nt_dot(wqT_ref[...], h) * scale).astype(BF16)
    kT = _nt_dot(wkT_ref[...], h)
    kT_ref[...] = kT
    k = kT.T.astype(BF16)
    for hp in range(N_HEADS // 2):
        k_ref[hp] = k[:, hp * LANES:(hp + 1) * LANES]
    vT = _nt_dot(wvT_ref[...], h)
    vT_ref[...] = vT
    vTb_ref[...] = vT.astype(BF16)
    lfT_ref[...] = _log_sigmoid(_nt_dot(wfT_ref[...], h) + bf_ref[...])


def _proj_prompt(x, g, wT, bf, *, batch, seq):
    m, d = x.shape
    tm = PROJ_TM
    nb = seq // tm
    n_pairs = N_HEADS // 2
    const = lambda i: (0, 0)
    tmaj = lambda i: (i // nb, 0, i % nb)
    return pl.pallas_call(
        functools.partial(_proj_prompt_kernel, scale=HEAD_DIM ** -0.5 * LOG2E),
        grid=(m // tm,),
        in_specs=[
            pl.BlockSpec((tm, d), lambda i: (i, 0)),
            pl.BlockSpec((1, d), const),
            *_qkvf_specs(d, lambda k: (lambda i: (k, 0))),
            pl.BlockSpec((N_HEADS, 1), const),
        ],
        out_specs=[
            pl.BlockSpec((None, d, tm), tmaj),
            pl.BlockSpec((n_pairs, tm, LANES), lambda i: (0, i, 0)),
            pl.BlockSpec((None, d, tm), tmaj),
            pl.BlockSpec((None, d, tm), tmaj),
            pl.BlockSpec((None, d, tm), tmaj),
            pl.BlockSpec((None, N_HEADS, tm), tmaj),
        ],
        out_shape=[
            jax.ShapeDtypeStruct((batch, d, seq), BF16),
            jax.ShapeDtypeStruct((n_pairs, m, LANES), BF16),
            jax.ShapeDtypeStruct((batch, d, seq), F32),
            jax.ShapeDtypeStruct((batch, d, seq), F32),
            jax.ShapeDtypeStruct((batch, d, seq), BF16),
            jax.ShapeDtypeStruct((batch, N_HEADS, seq), F32),
        ],
        compiler_params=_cparams(("arbitrary",)),
        name="proj_prompt",
    )(x, g, wT, wT, wT, wT, bf)


def _proj_sample_kernel(x_ref, g_ref, wqT_ref, wkT_ref, wvT_ref, wfT_ref, bf_ref,
                        qT_ref, kT_ref, vT_ref, lfT_ref, *, scale):
    h = _rms(x_ref[...], g_ref[...]).astype(BF16)
    qT_ref[...] = _nt_dot(wqT_ref[...], h) * scale
    kT_ref[...] = _nt_dot(wkT_ref[...], h)
    vT_ref[...] = _nt_dot(wvT_ref[...], h)
    lfT_ref[...] = _log_sigmoid(_nt_dot(wfT_ref[...], h) + bf_ref[...])


def _proj_sample(x, g, wT, bf):
    n, d = x.shape
    full = lambda shape: pl.BlockSpec(shape, lambda i: (0, 0))
    return pl.pallas_call(
        functools.partial(_proj_sample_kernel, scale=HEAD_DIM ** -0.5),
        grid=(1,),
        in_specs=[
            full((n, d)),
            full((1, d)),
            *_qkvf_specs(d, lambda k: (lambda i: (k, 0))),
            full((N_HEADS, 1)),
        ],
        out_specs=[full((d, n)), full((d, n)), full((d, n)), full((N_HEADS, n))],
        out_shape=[
            jax.ShapeDtypeStruct((d, n), F32),
            jax.ShapeDtypeStruct((d, n), F32),
            jax.ShapeDtypeStruct((d, n), F32),
            jax.ShapeDtypeStruct((N_HEADS, n), F32),
        ],
        compiler_params=_cparams(("arbitrary",)),
        name="proj_sample",
    )(x, g, wT, wT, wT, wT, bf)


def _cumsum_kernel(lfT_ref, cumT_ref, ka_ref, *, seq):
    blk = LANES
    n_pairs = N_HEADS // 2
    r = lax.broadcasted_iota(jnp.int32, (blk, blk), 0)
    c = lax.broadcasted_iota(jnp.int32, (blk, blk), 1)
    upper = (r <= c).astype(BF16)
    eye = (r == c).astype(BF16)
    slot = LANES // n_pairs
    head = lax.broadcasted_iota(jnp.int32, (N_HEADS, LANES), 0)
    col = lax.broadcasted_iota(jnp.int32, (N_HEADS, LANES), 1)
    target = (head >> 1) * slot + 3 + 3 * (head & 1)
    place = [(col == target + part).astype(BF16) for part in range(3)]
    lane = lax.broadcasted_iota(jnp.int32, (1, LANES), 1)
    ones_lanes = ((lane & (slot - 1)) < 3).astype(F32)
    carry = jnp.zeros((N_HEADS, 1), F32)
    for i in range(seq // blk):
        sl = slice(i * blk, (i + 1) * blk)
        cb = _dot3(lfT_ref[:, sl], upper) + carry
        cumT_ref[:, sl] = cb
        carry = cb[:, blk - 1:blk]
        cbt = _nt_dot3(eye, cb)
        hi, mid, lo = _split3(-(cbt * LOG2E))
        packed = _dot(hi, place[0]) + _dot(mid, place[1]) + _dot(lo, place[2]) + ones_lanes
        for hp in range(n_pairs):
            moved = packed if hp == 0 else pltpu.roll(packed, LANES - slot * hp, axis=1)
            ka_ref[hp, sl, :] = jnp.where(lane < slot, moved, 0.0).astype(BF16)


def _cumsum(lfT):
    batch, nh, seq = lfT.shape
    n_pairs = nh // 2
    return pl.pallas_call(
        functools.partial(_cumsum_kernel, seq=seq),
        grid=(batch,),
        in_specs=[pl.BlockSpec((None, nh, seq), lambda b: (b, 0, 0))],
        out_specs=[
            pl.BlockSpec((None, nh, seq), lambda b: (b, 0, 0)),
            pl.BlockSpec((n_pairs, seq, LANES), lambda b: (0, b, 0)),
        ],
        out_shape=[
            jax.ShapeDtypeStruct((batch, nh, seq), F32),
            jax.ShapeDtypeStruct((n_pairs, batch * seq, LANES), BF16),
        ],
        compiler_params=_cparams(("arbitrary",)),
        name="cumsum",
    )(lfT)


def _attn_kernel(qi_ref, kj_ref, k_ref, ka_ref, qT_ref, vT_ref, cumT_ref, o_ref,
                 qop_ref, m_ref, l_ref, al_ref, acc_ref, s_ref, p_ref, oT_ref, *, t):
    first = (pl.program_id(0) == 0) & (pl.program_id(1) == 0)
    i = qi_ref[pl.program_id(1)]
    j = kj_ref[pl.program_id(1)]
    n_pairs = N_HEADS // 2
    ch = ATTN_CH
    sub = 8

    @pl.when(first)
    def _():
        qop_ref[...] = jnp.zeros(qop_ref.shape, BF16)

    @pl.when(j == 0)
    def _():
        m_ref[...] = jnp.full(m_ref.shape, NEG_INF, F32)
        l_ref[...] = jnp.zeros(l_ref.shape, F32)
        acc_ref[...] = jnp.zeros(acc_ref.shape, F32)
        parts = [x.astype(F32) for x in _split3(cumT_ref[...] * LOG2E)]
        row16 = lax.broadcasted_iota(jnp.int32, (16, t), 0)
        for h in range(N_HEADS):
            e = h % 2
            own = slice(e * HEAD_DIM, (e + 1) * HEAD_DIM)
            qop_ref[h, own, :] = qT_ref[h // 2, own, :]
            hi, mid, lo = (x[h:h + 1, :] for x in parts)
            ones = jnp.where((row16 >= 3 + 3 * e) & (row16 < 6 + 3 * e), 1.0, 0.0)
            ext = jnp.where(row16 == 0, hi, jnp.where(row16 == 1, mid, jnp.where(row16 == 2, lo, ones)))
            qop_ref[h, LANES:LANES + 16, :] = ext.astype(BF16)

    def step(diagonal):
        if diagonal:
            row_minus_col = (lax.broadcasted_iota(jnp.int32, (ch, LANES), 0)
                             - lax.broadcasted_iota(jnp.int32, (ch, LANES), 1))
        n_ch, n_u = t // ch, t // LANES
        ones_rows = jnp.ones((16, t), BF16)

        def above(r, u):
            return diagonal and r * ch > u * LANES + LANES - 1

        def crosses(r, u):
            return diagonal and not above(r, u) and r * ch + ch - 1 > u * LANES

        def scores(hp, slot):
            kx = jnp.concatenate([k_ref[hp], ka_ref[hp]], axis=1)
            for e in range(2):
                s_ref[slot, e] = _dot(kx, qop_ref[2 * hp + e])

        def softmax(hp, slot):
            for e in range(2):
                h = 2 * hp + e
                for u in range(n_u):
                    lsl = slice(u * LANES, (u + 1) * LANES)
                    mx = None
                    for r in range(n_ch):
                        if above(r, u):
                            continue
                        rows = slice(r * ch, (r + 1) * ch)
                        blk = s_ref[slot, e, rows, lsl]
                        if crosses(r, u):
                            blk = jnp.where(row_minus_col <= (u * LANES - r * ch), blk, NEG_INF)
                            s_ref[slot, e, rows, lsl] = blk
                        part = jnp.max(blk.reshape(ch // sub, sub, LANES), axis=0)
                        mx = part if mx is None else jnp.maximum(mx, part)
                    m_old = m_ref[h, :, lsl]
                    m_new = jnp.maximum(m_old, jnp.max(mx, axis=0, keepdims=True))
                    al_ref[slot, e, :, lsl] = jnp.exp2(m_old - m_new)
                    m_ref[h, :, lsl] = m_new
                    for r in range(n_ch):
                        rows = slice(r * ch, (r + 1) * ch)
                        if above(r, u):
                            p_ref[slot, e, rows, lsl] = jnp.zeros((ch, LANES), BF16)
                            continue
                        blk = s_ref[slot, e, rows, lsl].reshape(ch // sub, sub, LANES)
                        p = jnp.exp2(blk - m_new[None])
                        p_ref[slot, e, rows, lsl] = p.reshape(ch, LANES).astype(BF16)

        def values(hp, slot):
            for e in range(2):
                h = 2 * hp + e
                vx = jnp.concatenate([vT_ref[h], ones_rows], axis=0)
                c = _dot(vx, p_ref[slot, e])
                al = al_ref[slot, e]
                acc = acc_ref[h].reshape(HEAD_DIM // sub, sub, t) * al[None]
                acc_ref[h] = acc.reshape(HEAD_DIM, t) + c[0:HEAD_DIM]
                l_ref[h] = al * l_ref[h] + c[HEAD_DIM:HEAD_DIM + sub]

        scores(0, 0)
        for k in range(n_pairs):
            if k + 1 < n_pairs:
                scores(k + 1, (k + 1) % 2)
            softmax(k, k % 2)
            values(k, k % 2)

    @pl.when(j < i)
    def _():
        step(False)

    @pl.when(j == i)
    def _():
        step(True)
        for h in range(N_HEADS):
            o = acc_ref[h].reshape(HEAD_DIM // sub, sub, t) / l_ref[h][None]
            oT_ref[h * HEAD_DIM:(h + 1) * HEAD_DIM, :] = o.reshape(HEAD_DIM, t)
        o_ref[...] = oT_ref[...].T.astype(BF16)


def _attention(k, ka, qT, vTb, cumT, *, batch, seq):
    n_pairs, m, _ = k.shape
    d = N_HEADS * HEAD_DIM
    t = ATTN_T
    nq = seq // t
    pairs = [(i, j) for i in range(nq) for j in range(i + 1)]
    qi = jnp.array([i for i, _ in pairs], jnp.int32)
    kj = jnp.array([j for _, j in pairs], jnp.int32)
    kvrow = lambda b, s, qi, kj: (0, b * nq + kj[s], 0)
    grid_spec = pltpu.PrefetchScalarGridSpec(
        num_scalar_prefetch=2,
        grid=(batch, len(pairs)),
        in_specs=[
            pl.BlockSpec((n_pairs, t, LANES), kvrow),
            pl.BlockSpec((n_pairs, t, LANES), kvrow),
            pl.BlockSpec((None, n_pairs, LANES, t), lambda b, s, qi, kj: (b, 0, 0, qi[s])),
            pl.BlockSpec((None, N_HEADS, HEAD_DIM, t), lambda b, s, qi, kj: (b, 0, 0, kj[s])),
            pl.BlockSpec((None, N_HEADS, t), lambda b, s, qi, kj: (b, 0, qi[s])),
        ],
        out_specs=pl.BlockSpec((t, d), lambda b, s, qi, kj: (b * nq + qi[s], 0)),
        scratch_shapes=[
            pltpu.VMEM((N_HEADS, 2 * LANES, t), BF16),
            pltpu.VMEM((N_HEADS, 8, t), F32),
            pltpu.VMEM((N_HEADS, 8, t), F32),
            pltpu.VMEM((2, 2, 8, t), F32),
            pltpu.VMEM((N_HEADS, HEAD_DIM, t), F32),
            pltpu.VMEM((2, 2, t, t), F32),
            pltpu.VMEM((2, 2, t, t), BF16),
            pltpu.VMEM((d, t), F32),
        ],
    )
    return pl.pallas_call(
        functools.partial(_attn_kernel, t=t),
        grid_spec=grid_spec,
        out_shape=jax.ShapeDtypeStruct((m, d), BF16),
        compiler_params=_cparams(("arbitrary", "arbitrary")),
        name="attention",
    )(qi, kj, k, ka, qT.reshape(batch, n_pairs, LANES, seq), vTb.reshape(batch, N_HEADS, HEAD_DIM, seq), cumT)


def _decode_init(b, n_req, qT_ref, kTs_ref, vTs_ref, lfTs_ref, state):
    qb_ref, acc_ref, m_ref, l_ref, carry_ref = state[:5]
    sel = lax.broadcasted_iota(jnp.int32, (1, n_req), 1) == b
    pick = lambda ref: jnp.sum(jnp.where(sel, ref[...], 0.0), axis=1, keepdims=True)
    qcol = pick(qT_ref)
    kcol = pick(kTs_ref)
    vcol = pick(vTs_ref)
    qk = qcol * kcol
    lane0 = lax.broadcasted_iota(jnp.int32, (1, LANES), 1) == 0
    for h in range(N_HEADS):
        hs = slice(h * HEAD_DIM, (h + 1) * HEAD_DIM)
        qb_ref[h] = jnp.broadcast_to(qcol[hs], (HEAD_DIM, LANES))
        acc_ref[h] = jnp.where(lane0, vcol[hs], 0.0)
        m_ref[h:h + 1, :] = jnp.sum(qk[hs], axis=0, keepdims=True)
    l_ref[...] = jnp.ones(l_ref.shape, F32)
    carry_ref[...] = pick(lfTs_ref)


def _decode_scores(k_refs, state):
    qb_ref, s_ref = state[0], state[5]
    for h in range(N_HEADS):
        qb = qb_ref[h]
        for g, k_ref in enumerate(k_refs):
            s_ref[g, h:h + 1, :] = jnp.sum(k_ref[h] * qb, axis=0, keepdims=True)


def _decode_probs(lf_refs, state):
    _, _, m_ref, l_ref, carry_ref, s_ref, p_ref, ab_ref = state
    r = lax.broadcasted_iota(jnp.int32, (PAGE_SIZE, PAGE_SIZE), 0)
    c = lax.broadcasted_iota(jnp.int32, (PAGE_SIZE, PAGE_SIZE), 1)
    after = (r > c).astype(BF16)
    carry = carry_ref[...]
    logits = []
    for g, lf_ref in enumerate(lf_refs):
        lfT = lf_ref[...]
        logits.append(s_ref[g] + (_dot3(lfT, after) + carry))
        carry = carry + jnp.sum(lfT, axis=1, keepdims=True)
    carry_ref[...] = carry
    m_prev = m_ref[...]
    m_next = jnp.maximum(m_prev, jnp.max(functools.reduce(jnp.maximum, logits), axis=1, keepdims=True))
    alpha = jnp.exp(m_prev - m_next)
    psum = None
    for g in range(len(lf_refs)):
        p = jnp.exp(logits[g] - m_next)
        p_ref[g] = p
        psum = p if psum is None else psum + p
    l_ref[...] = alpha * l_ref[...] + jnp.sum(psum, axis=1, keepdims=True)
    m_ref[...] = m_next
    ab_ref[...] = jnp.broadcast_to(alpha, ab_ref.shape)


def _decode_values(v_refs, state):
    acc_ref, p_ref, ab_ref = state[1], state[6], state[7]
    for h in range(N_HEADS):
        acc = ab_ref[h:h + 1, :] * acc_ref[h]
        for g, v_ref in enumerate(v_refs):
            acc = acc + v_ref[h] * p_ref[g, h:h + 1, :]
        acc_ref[h] = acc


def _decode_finish(o_ref, state):
    acc_ref, l_ref = state[1], state[3]
    for h in range(N_HEADS):
        acc_ref[h] = acc_ref[h] / l_ref[h:h + 1, :]
    ones = jnp.ones((8, LANES), BF16)
    o8 = _nt_dot3(ones, acc_ref[...].reshape(N_HEADS * HEAD_DIM, LANES))
    o_ref[...] = o8[0:1, :].astype(BF16)


def _decode_scratch(group):
    return [
        pltpu.VMEM((N_HEADS, HEAD_DIM, LANES), F32),
        pltpu.VMEM((N_HEADS, HEAD_DIM, LANES), F32),
        pltpu.VMEM((N_HEADS, 1), F32),
        pltpu.VMEM((N_HEADS, 1), F32),
        pltpu.VMEM((N_HEADS, 1), F32),
        pltpu.VMEM((group, N_HEADS, PAGE_SIZE), F32),
        pltpu.VMEM((group, N_HEADS, PAGE_SIZE), F32),
        pltpu.VMEM((N_HEADS, LANES), F32),
    ]


def _decode_specs(n_req, group, page_of):
    d = N_HEADS * HEAD_DIM
    small = lambda *a: (0, 0)
    kv_block = (None, N_HEADS, HEAD_DIM, PAGE_SIZE)
    kv_map = lambda g: (lambda *a: (page_of(g, *a), 0, 0, 0))
    lf_map = lambda g: (lambda *a: (page_of(g, *a), 0, 0))
    specs = [pl.BlockSpec((d, n_req), small)] * 3 + [pl.BlockSpec((N_HEADS, n_req), small)]
    specs += [pl.BlockSpec(kv_block, kv_map(g)) for g in range(group)]
    specs += [pl.BlockSpec(kv_block, kv_map(g)) for g in range(group)]
    specs += [pl.BlockSpec((None, N_HEADS, PAGE_SIZE), lf_map(g)) for g in range(group)]
    return specs


def _ffn_kernel(*refs, n_chunks, final_norm, dec):
    if dec is None:
        x_ref, z_ref, wmo_ref, g_ref, wg_ref, wu_ref, wo_ref, gf_ref, out_ref, acc_ref, h_ref = refs
    else:
        n_req, group, spr = dec
        x_ref, z_ref, wmo_ref, g_ref, wg_ref, wu_ref, wo_ref, gf_ref = refs[1:9]
        qT_ref, kTs_ref, vTs_ref, lfTs_ref = refs[9:13]
        k_refs = refs[13:13 + group]
        v_refs = refs[13 + group:13 + 2 * group]
        lf_refs = refs[13 + 2 * group:13 + 3 * group]
        out_ref, o_dec_ref, acc_ref, h_ref = refs[13 + 3 * group:17 + 3 * group]
        state = refs[17 + 3 * group:]
        step = pl.program_id(0) * n_chunks + pl.program_id(1)
    c = pl.program_id(1)

    @pl.when(c == 0)
    def _():
        y1 = x_ref[...] + _dot(z_ref[...], wmo_ref[...])
        acc_ref[...] = y1
        h_ref[...] = _rms(y1, g_ref[...]).astype(BF16)

    if dec is not None:
        active = step < n_req * spr

        @pl.when((step % spr == 0) & active)
        def _():
            _decode_init(step // spr, n_req, qT_ref, kTs_ref, vTs_ref, lfTs_ref, state)

    def chunk(with_pages):
        if with_pages:
            _decode_scores(k_refs, state)
            _decode_probs(lf_refs, state)
            _decode_values(v_refs, state)
        h = h_ref[...]
        gate = _dot(h, wg_ref[...])
        up = _dot(h, wu_ref[...])
        a = (gate * jax.nn.sigmoid(gate) * up).astype(BF16)
        acc_ref[...] += _dot(a, wo_ref[...])

    if dec is None:
        chunk(False)
    else:
        pl.when(active)(lambda: chunk(True))
        pl.when(jnp.logical_not(active))(lambda: chunk(False))

    @pl.when(c == n_chunks - 1)
    def _():
        y = acc_ref[...]
        out_ref[...] = _rms(y, gf_ref[...]) if final_norm else y

    if dec is not None:
        @pl.when((step % spr == spr - 1) & active)
        def _():
            _decode_finish(o_dec_ref, state)


def _mix_ffn(x, z, wmo, g, w_in, w_out, g_final, *, layer, tm, tc, final_norm, decode=None):
    m, d = x.shape
    d_ff = w_out.shape[1]
    n_chunks = d_ff // tc
    row = lambda i, c, *pt: (i, 0)
    const = lambda i, c, *pt: (0, 0)
    in_specs = [
        pl.BlockSpec((tm, d), row),
        pl.BlockSpec((tm, d), row),
        pl.BlockSpec((d, d), const),
        pl.BlockSpec((1, d), const),
        pl.BlockSpec((None, d, tc), lambda i, c, *pt: (layer, 0, c)),
        pl.BlockSpec((None, d, tc), lambda i, c, *pt: (layer, 0, n_chunks + c)),
        pl.BlockSpec((None, tc, d), lambda i, c, *pt: (layer, c, 0)),
        pl.BlockSpec((1, d), const),
    ]
    scratch = [pltpu.VMEM((tm, d), F32), pltpu.VMEM((tm, d), BF16)]
    operands = (x, z, wmo, g, w_in, w_in, w_out, g_final)
    grid = (m // tm, n_chunks)
    y_spec = pl.BlockSpec((tm, d), row)
    y_shape = jax.ShapeDtypeStruct((m, d), F32)
    if decode is None:
        return pl.pallas_call(
            functools.partial(_ffn_kernel, n_chunks=n_chunks, final_norm=final_norm, dec=None),
            grid=grid, in_specs=in_specs, out_specs=y_spec, out_shape=y_shape,
            scratch_shapes=scratch,
            compiler_params=_cparams(("arbitrary", "arbitrary")),
            name="mix_ffn",
        )(*operands)

    page_table, qT, kTs, vTs, lfTs, kT_cache, vT_cache, lfT_cache = decode
    n_req, n_pages = page_table.shape
    group = FFN_DEC_G
    spr = n_pages // group
    assert grid[0] * grid[1] >= n_req * spr, "not enough grid steps for these requests"

    def request_of(i, c):
        step = i * n_chunks + c
        valid = step < n_req * spr
        return jnp.where(valid, step // spr, n_req - 1), jnp.where(valid, step % spr, spr - 1)

    def page_of(g, i, c, pt):
        req, part = request_of(i, c)
        return pt[req, n_pages - 1 - (part * group + g)]

    grid_spec = pltpu.PrefetchScalarGridSpec(
        num_scalar_prefetch=1,
        grid=grid,
        in_specs=in_specs + _decode_specs(n_req, group, page_of),
        out_specs=[y_spec, pl.BlockSpec((None, 1, d), lambda i, c, pt: (request_of(i, c)[0], 0, 0))],
        scratch_shapes=scratch + _decode_scratch(group),
    )
    y, o_dec = pl.pallas_call(
        functools.partial(_ffn_kernel, n_chunks=n_chunks, final_norm=final_norm, dec=(n_req, group, spr)),
        grid_spec=grid_spec,
        out_shape=[y_shape, jax.ShapeDtypeStruct((n_req, 1, d), BF16)],
        compiler_params=_cparams(("arbitrary", "arbitrary")),
        name="mix_ffn_decode",
    )(page_table, *operands, qT, kTs, vTs, lfTs,
      *([kT_cache] * group), *([vT_cache] * group), *([lfT_cache] * group))
    return y, o_dec.reshape(n_req, d)


def _conv_prompt_kernel(x_ref, g_ref, wb_ref, wc_ref, wu_ref, ck_ref, prev_ref, z_ref, st_ref,
                        buf_ref, *, tm, blocks_per_seq):
    i = pl.program_id(0)
    h = _rms(x_ref[...], g_ref[...]).astype(BF16)
    gate = _dot(h, wb_ref[...])
    cu = _dot(h, wc_ref[...]) * _dot(h, wu_ref[...])

    @pl.when(i % blocks_per_seq == 0)
    def _():
        buf_ref[6:8, :] = prev_ref[...]

    buf_ref[8:8 + tm, :] = cu
    y = ck_ref[0:1, :] * buf_ref[6:6 + tm, :] + ck_ref[1:2, :] * buf_ref[7:7 + tm, :]
    y = y + ck_ref[2:3, :] * cu
    z_ref[...] = (gate * y).astype(BF16)
    tail = cu[tm - 2:tm, :]
    st_ref[...] = tail
    buf_ref[6:8, :] = tail


def _conv_prompt(x, g, w_in, ck, prev, *, batch, seq):
    m, d = x.shape
    tm = CONV_TM
    bps = seq // tm
    const = lambda i: (0, 0)
    return pl.pallas_call(
        functools.partial(_conv_prompt_kernel, tm=tm, blocks_per_seq=bps),
        grid=(m // tm,),
        in_specs=[
            pl.BlockSpec((tm, d), lambda i: (i, 0)),
            pl.BlockSpec((1, d), const),
            pl.BlockSpec((d, d), lambda i: (0, 0)),
            pl.BlockSpec((d, d), lambda i: (0, 1)),
            pl.BlockSpec((d, d), lambda i: (0, 2)),
            pl.BlockSpec((3, d), const),
            pl.BlockSpec((None, 2, d), lambda i: (i // bps, 0, 0)),
        ],
        out_specs=[
            pl.BlockSpec((tm, d), lambda i: (i, 0)),
            pl.BlockSpec((None, 2, d), lambda i: (i // bps, 0, 0)),
        ],
        out_shape=[
            jax.ShapeDtypeStruct((m, d), BF16),
            jax.ShapeDtypeStruct((batch, 2, d), F32),
        ],
        scratch_shapes=[pltpu.VMEM((tm + 8, d), F32)],
        compiler_params=_cparams(("arbitrary",)),
        name="conv_prompt",
    )(x, g, w_in, w_in, w_in, ck, prev)


def _conv_sample_kernel(x_ref, g_ref, wb_ref, wc_ref, wu_ref, ck_ref, p0_ref, p1_ref, z_ref, cu_ref):
    h = _rms(x_ref[...], g_ref[...]).astype(BF16)
    gate = _dot(h, wb_ref[...])
    cu = _dot(h, wc_ref[...]) * _dot(h, wu_ref[...])
    y = ck_ref[0:1, :] * p0_ref[...] + ck_ref[1:2, :] * p1_ref[...]
    y = y + ck_ref[2:3, :] * cu
    z_ref[...] = (gate * y).astype(BF16)
    cu_ref[...] = cu


def _conv_sample(x, g, w_in, ck, p0, p1):
    n, d = x.shape
    full = lambda shape: pl.BlockSpec(shape, lambda i: (0, 0))
    return pl.pallas_call(
        _conv_sample_kernel,
        grid=(1,),
        in_specs=[
            full((n, d)),
            full((1, d)),
            pl.BlockSpec((d, d), lambda i: (0, 0)),
            pl.BlockSpec((d, d), lambda i: (0, 1)),
            pl.BlockSpec((d, d), lambda i: (0, 2)),
            full((3, d)),
            full((n, d)),
            full((n, d)),
        ],
        out_specs=[full((n, d)), full((n, d))],
        out_shape=[jax.ShapeDtypeStruct((n, d), BF16), jax.ShapeDtypeStruct((n, d), F32)],
        compiler_params=_cparams(("arbitrary",)),
        name="conv_sample",
    )(x, g, w_in, w_in, w_in, ck, p0, p1)


def kernel(x_prompt, x_sample, cache_k, cache_v, cache_logf, state_conv, page_table,
           norm_mix, norm_ffn, w_ffn_in, w_ffn_out, norm_final,
           attn_w_qkvf, attn_b_f, attn_w_o, conv_w_in, conv_kernel, conv_w_out):
    batch, seq, d = x_prompt.shape
    n_req = x_sample.shape[0]
    assert x_sample.shape[1] == 1 and d == N_HEADS * HEAD_DIM
    assert norm_mix.shape[0] == 2 and cache_k.shape[2] == PAGE_SIZE

    xp = x_prompt.reshape(batch * seq, d)
    xs = x_sample.reshape(n_req, d)
    gfin = norm_final.reshape(1, d)

    wT = attn_w_qkvf[0].T.astype(BF16)
    bf = attn_b_f[0].reshape(N_HEADS, 1)
    g0 = norm_mix[0].reshape(1, d)

    qT, k, kT, vT, vTb, lfT = _proj_prompt(xp, g0, wT, bf, batch=batch, seq=seq)
    cumT, ka = _cumsum(lfT)
    o_p = _attention(k, ka, qT, vTb, cumT, batch=batch, seq=seq)

    qTs, kTs, vTs, lfTs = _proj_sample(xs, g0, wT, bf)
    caches = (jnp.transpose(cache_k[0], (0, 2, 3, 1)),
              jnp.transpose(cache_v[0], (0, 2, 3, 1)),
              jnp.transpose(cache_logf[0], (0, 2, 1)))
    n_pages = page_table.shape[1]
    ffn_steps = (batch * seq // FFN_TM) * (w_ffn_out.shape[1] // FFN_TC)
    capacity = ffn_steps // (n_pages // FFN_DEC_G)
    n_ride = (n_req + 1) // 2
    assert n_ride <= capacity

    def requests(lo, hi):
        return (page_table[lo:hi], qTs[:, lo:hi], kTs[:, lo:hi], vTs[:, lo:hi], lfTs[:, lo:hi], *caches)

    wo = attn_w_o[0].astype(BF16)
    w_in, w_out = w_ffn_in.astype(BF16), w_ffn_out.astype(BF16)
    gf0 = norm_ffn[0].reshape(1, d)
    yp, o_s0 = _mix_ffn(xp, o_p, wo, gf0, w_in, w_out, gfin, layer=0, tm=FFN_TM, tc=FFN_TC,
                        final_norm=False, decode=requests(0, n_ride))

    wci = conv_w_in[0].astype(BF16)
    g1 = norm_mix[1].reshape(1, d)
    ck = conv_kernel[0]
    wco = conv_w_out[0].astype(BF16)
    gf1 = norm_ffn[1].reshape(1, d)
    zp, conv_p = _conv_prompt(yp, g1, wci, ck, jnp.zeros((batch, 2, d), F32), batch=batch, seq=seq)
    yp, o_s1 = _mix_ffn(yp, zp, wco, gf1, w_in, w_out, gfin, layer=1, tm=FFN_TM, tc=FFN_TC,
                        final_norm=True, decode=requests(n_ride, n_req))

    o_s = jnp.concatenate([o_s0, o_s1], axis=0)
    tc_s = w_ffn_out.shape[1] // 2
    ys = _mix_ffn(xs, o_s, wo, gf0, w_in, w_out, gfin, layer=0, tm=n_req, tc=tc_s, final_norm=False)
    zs, cu_s = _conv_sample(ys, g1, wci, ck, state_conv[0, :, 0], state_conv[0, :, 1])
    ys = _mix_ffn(ys, zs, wco, gf1, w_in, w_out, gfin, layer=1, tm=n_req, tc=tc_s, final_norm=True)

    to_heads = lambda t: jnp.transpose(t.reshape(batch, N_HEADS, HEAD_DIM, seq), (0, 3, 1, 2))[None]
    to_heads_s = lambda t: t.T.reshape(1, n_req, 1, N_HEADS, HEAD_DIM)
    return (
        yp.reshape(batch, seq, d),
        ys.reshape(n_req, 1, d),
        to_heads(kT),
        to_heads(vT),
        jnp.transpose(lfT, (0, 2, 1))[None],
        to_heads_s(kTs),
        to_heads_s(vTs),
        lfTs.T.reshape(1, n_req, 1, N_HEADS),
        conv_p[None],
        jnp.stack([state_conv[0, :, 1], cu_s], axis=1)[None],
    )
```

```python
import functools

import jax
import jax.numpy as jnp
from jax import lax
from jax.experimental import pallas as pl
from jax.experimental.pallas import tpu as pltpu

F32 = jnp.float32
BF16 = jnp.bfloat16

RMS_EPS = 1e-6
NEG_INF = -1e30
N_HEADS = 16
HEAD_DIM = 64
PAGE_SIZE = 128
LANES = 128
VMEM_LIMIT = 56 * 1024 * 1024

PROJ_TM = 512
ATTN_T = 512
ATTN_CH = 64
LOG2E = 1.4426950408889634
FFN_TM = 1024
FFN_TC = 256
CONV_TM = 1024
FFN_DEC_G = 8


def _cparams(sem):
    return pltpu.CompilerParams(dimension_semantics=sem, vmem_limit_bytes=VMEM_LIMIT)


def _rms(x, g):
    ms = jnp.mean(x * x, axis=-1, keepdims=True)
    return x * lax.rsqrt(ms + RMS_EPS) * g


def _dot(a, b):
    return jnp.dot(a, b, preferred_element_type=F32)


def _nt_dot(a, b):
    return lax.dot_general(a, b, (((1,), (1,)), ((), ())), preferred_element_type=F32)


def _log_sigmoid(x):
    return jnp.minimum(x, 0.0) - jnp.log1p(jnp.exp(-jnp.abs(x)))


def _split3(x):
    hi = x.astype(BF16)
    r1 = x - hi.astype(F32)
    mid = r1.astype(BF16)
    lo = (r1 - mid.astype(F32)).astype(BF16)
    return hi, mid, lo


def _dot3(x, w):
    hi, mid, lo = _split3(x)
    return _dot(hi, w) + _dot(mid, w) + _dot(lo, w)


def _nt_dot3(w, x):
    hi, mid, lo = _split3(x)
    return _nt_dot(w, hi) + _nt_dot(w, mid) + _nt_dot(w, lo)


def _qkvf_specs(d, index):
    return [
        pl.BlockSpec((d, d), index(0)),
        pl.BlockSpec((d, d), index(1)),
        pl.BlockSpec((d, d), index(2)),
        pl.BlockSpec((N_HEADS, d), index(3 * d // N_HEADS)),
    ]


def _proj_prompt_kernel(x_ref, g_ref, wqT_ref, wkT_ref, wvT_ref, wfT_ref, bf_ref,
                        qT_ref, k_ref, kT_ref, vT_ref, vTb_ref, lfT_ref, *, scale):
    h = _rms(x_ref[...], g_ref[...]).astype(BF16)
    qT_ref[...] = (_nt_dot(wqT_ref[...], h) * scale).astype(BF16)
    kT = _nt_dot(wkT_ref[...], h)
    kT_ref[...] = kT
    k = kT.T.astype(BF16)
    for hp in range(N_HEADS // 2):
        k_ref[hp] = k[:, hp * LANES:(hp + 1) * LANES]
    vT = _nt_dot(wvT_ref[...], h)
    vT_ref[...] = vT
    vTb_ref[...] = vT.astype(BF16)
    lfT_ref[...] = _log_sigmoid(_nt_dot(wfT_ref[...], h) + bf_ref[...])


def _proj_prompt(x, g, wT, bf, *, batch, seq):
    m, d = x.shape
    tm = PROJ_TM
    nb = seq // tm
    n_pairs = N_HEADS // 2
    const = lambda i: (0, 0)
    tmaj = lambda i: (i // nb, 0, i % nb)
    return pl.pallas_call(
        functools.partial(_proj_prompt_kernel, scale=HEAD_DIM ** -0.5 * LOG2E),
        grid=(m // tm,),
        in_specs=[
            pl.BlockSpec((tm, d), lambda i: (i, 0)),
            pl.BlockSpec((1, d), const),
            *_qkvf_specs(d, lambda k: (lambda i: (k, 0))),
            pl.BlockSpec((N_HEADS, 1), const),
        ],
        out_specs=[
            pl.BlockSpec((None, d, tm), tmaj),
            pl.BlockSpec((n_pairs, tm, LANES), lambda i: (0, i, 0)),
            pl.BlockSpec((None, d, tm), tmaj),
            pl.BlockSpec((None, d, tm), tmaj),
            pl.BlockSpec((None, d, tm), tmaj),
            pl.BlockSpec((None, N_HEADS, tm), tmaj),
        ],
        out_shape=[
            jax.ShapeDtypeStruct((batch, d, seq), BF16),
            jax.ShapeDtypeStruct((n_pairs, m, LANES), BF16),
            jax.ShapeDtypeStruct((batch, d, seq), F32),
            jax.ShapeDtypeStruct((batch, d, seq), F32),
            jax.ShapeDtypeStruct((batch, d, seq), BF16),
            jax.ShapeDtypeStruct((batch, N_HEADS, seq), F32),
        ],
        compiler_params=_cparams(("arbitrary",)),
        name="proj_prompt",
    )(x, g, wT, wT, wT, wT, bf)


def _proj_sample_kernel(x_ref, g_ref, wqT_ref, wkT_ref, wvT_ref, wfT_ref, bf_ref,
                        qT_ref, kT_ref, vT_ref, lfT_ref, *, scale):
    h = _rms(x_ref[...], g_ref[...]).astype(BF16)
    qT_ref[...] = _nt_dot(wqT_ref[...], h) * scale
    kT_ref[...] = _nt_dot(wkT_ref[...], h)
    vT_ref[...] = _nt_dot(wvT_ref[...], h)
    lfT_ref[...] = _log_sigmoid(_nt_dot(wfT_ref[...], h) + bf_ref[...])


def _proj_sample(x, g, wT, bf):
    n, d = x.shape
    full = lambda shape: pl.BlockSpec(shape, lambda i: (0, 0))
    return pl.pallas_call(
        functools.partial(_proj_sample_kernel, scale=HEAD_DIM ** -0.5),
        grid=(1,),
        in_specs=[
            full((n, d)),
            full((1, d)),
            *_qkvf_specs(d, lambda k: (lambda i: (k, 0))),
            full((N_HEADS, 1)),
        ],
        out_specs=[full((d, n)), full((d, n)), full((d, n)), full((N_HEADS, n))],
        out_shape=[
            jax.ShapeDtypeStruct((d, n), F32),
            jax.ShapeDtypeStruct((d, n), F32),
            jax.ShapeDtypeStruct((d, n), F32),
            jax.ShapeDtypeStruct((N_HEADS, n), F32),
        ],
        compiler_params=_cparams(("arbitrary",)),
        name="proj_sample",
    )(x, g, wT, wT, wT, wT, bf)


def _cumsum_kernel(lfT_ref, cumT_ref, ka_ref, *, seq):
    blk = LANES
    n_pairs = N_HEADS // 2
    r = lax.broadcasted_iota(jnp.int32, (blk, blk), 0)
    c = lax.broadcasted_iota(jnp.int32, (blk, blk), 1)
    upper = (r <= c).astype(BF16)
    eye = (r == c).astype(BF16)
    slot = LANES // n_pairs
    head = lax.broadcasted_iota(jnp.int32, (N_HEADS, LANES), 0)
    col = lax.broadcasted_iota(jnp.int32, (N_HEADS, LANES), 1)
    target = (head >> 1) * slot + 3 + 3 * (head & 1)
    place = [(col == target + part).astype(BF16) for part in range(3)]
    lane = lax.broadcasted_iota(jnp.int32, (1, LANES), 1)
    ones_lanes = ((lane & (slot - 1)) < 3).astype(F32)
    carry = jnp.zeros((N_HEADS, 1), F32)
    for i in range(seq // blk):
        sl = slice(i * blk, (i + 1) * blk)
        cb = _dot3(lfT_ref[:, sl], upper) + carry
        cumT_ref[:, sl] = cb
        carry = cb[:, blk - 1:blk]
        cbt = _nt_dot3(eye, cb)
        hi, mid, lo = _split3(-(cbt * LOG2E))
        packed = _dot(hi, place[0]) + _dot(mid, place[1]) + _dot(lo, place[2]) + ones_lanes
        for hp in range(n_pairs):
            moved = packed if hp == 0 else pltpu.roll(packed, LANES - slot * hp, axis=1)
            ka_ref[hp, sl, :] = jnp.where(lane < slot, moved, 0.0).astype(BF16)


def _cumsum(lfT):
    batch, nh, seq = lfT.shape
    n_pairs = nh // 2
    return pl.pallas_call(
        functools.partial(_cumsum_kernel, seq=seq),
        grid=(batch,),
        in_specs=[pl.BlockSpec((None, nh, seq), lambda b: (b, 0, 0))],
        out_specs=[
            pl.BlockSpec((None, nh, seq), lambda b: (b, 0, 0)),
            pl.BlockSpec((n_pairs, seq, LANES), lambda b: (0, b, 0)),
        ],
        out_shape=[
            jax.ShapeDtypeStruct((batch, nh, seq), F32),
            jax.ShapeDtypeStruct((n_pairs, batch * seq, LANES), BF16),
        ],
        compiler_params=_cparams(("arbitrary",)),
        name="cumsum",
    )(lfT)


def _attn_kernel(qi_ref, kj_ref, k_ref, ka_ref, qT_ref, vT_ref, cumT_ref, w1_ref, w2_ref,
                 o_ref, w1b_ref, w2b_ref,
                 qop_ref, m_ref, l_ref, al_ref, acc_ref, s_ref, p_ref, oT_ref, *, t):
    w1b_ref[...] = w1_ref[...].astype(BF16)
    w2b_ref[...] = w2_ref[...].astype(BF16)
    first = (pl.program_id(0) == 0) & (pl.program_id(1) == 0)
    i = qi_ref[pl.program_id(1)]
    j = kj_ref[pl.program_id(1)]
    n_pairs = N_HEADS // 2
    ch = ATTN_CH
    sub = 8

    @pl.when(first)
    def _():
        qop_ref[...] = jnp.zeros(qop_ref.shape, BF16)

    @pl.when(j == 0)
    def _():
        m_ref[...] = jnp.full(m_ref.shape, NEG_INF, F32)
        l_ref[...] = jnp.zeros(l_ref.shape, F32)
        acc_ref[...] = jnp.zeros(acc_ref.shape, F32)
        parts = [x.astype(F32) for x in _split3(cumT_ref[...] * LOG2E)]
        row16 = lax.broadcasted_iota(jnp.int32, (16, t), 0)
        for h in range(N_HEADS):
            e = h % 2
            own = slice(e * HEAD_DIM, (e + 1) * HEAD_DIM)
            qop_ref[h, own, :] = qT_ref[h // 2, own, :]
            hi, mid, lo = (x[h:h + 1, :] for x in parts)
            ones = jnp.where((row16 >= 3 + 3 * e) & (row16 < 6 + 3 * e), 1.0, 0.0)
            ext = jnp.where(row16 == 0, hi, jnp.where(row16 == 1, mid, jnp.where(row16 == 2, lo, ones)))
            qop_ref[h, LANES:LANES + 16, :] = ext.astype(BF16)

    def step(diagonal):
        if diagonal:
            row_minus_col = (lax.broadcasted_iota(jnp.int32, (ch, LANES), 0)
                             - lax.broadcasted_iota(jnp.int32, (ch, LANES), 1))
        n_ch, n_u = t // ch, t // LANES
        ones_rows = jnp.ones((16, t), BF16)

        def above(r, u):
            return diagonal and r * ch > u * LANES + LANES - 1

        def crosses(r, u):
            return diagonal and not above(r, u) and r * ch + ch - 1 > u * LANES

        def scores(hp, slot):
            kx = jnp.concatenate([k_ref[hp], ka_ref[hp]], axis=1)
            for e in range(2):
                s_ref[slot, e] = _dot(kx, qop_ref[2 * hp + e])

        def softmax(hp, slot):
            for e in range(2):
                h = 2 * hp + e
                for u in range(n_u):
                    lsl = slice(u * LANES, (u + 1) * LANES)
                    mx = None
                    for r in range(n_ch):
                        if above(r, u):
                            continue
                        rows = slice(r * ch, (r + 1) * ch)
                        blk = s_ref[slot, e, rows, lsl]
                        if crosses(r, u):
                            blk = jnp.where(row_minus_col <= (u * LANES - r * ch), blk, NEG_INF)
                            s_ref[slot, e, rows, lsl] = blk
                        part = jnp.max(blk.reshape(ch // sub, sub, LANES), axis=0)
                        mx = part if mx is None else jnp.maximum(mx, part)
                    m_old = m_ref[h, :, lsl]
                    m_new = jnp.maximum(m_old, jnp.max(mx, axis=0, keepdims=True))
                    al_ref[slot, e, :, lsl] = jnp.exp2(m_old - m_new)
                    m_ref[h, :, lsl] = m_new
                    for r in range(n_ch):
                        rows = slice(r * ch, (r + 1) * ch)
                        if above(r, u):
                            p_ref[slot, e, rows, lsl] = jnp.zeros((ch, LANES), BF16)
                            continue
                        blk = s_ref[slot, e, rows, lsl].reshape(ch // sub, sub, LANES)
                        p = jnp.exp2(blk - m_new[None])
                        p_ref[slot, e, rows, lsl] = p.reshape(ch, LANES).astype(BF16)

        def values(hp, slot):
            for e in range(2):
                h = 2 * hp + e
                vx = jnp.concatenate([vT_ref[h], ones_rows], axis=0)
                c = _dot(vx, p_ref[slot, e])
                al = al_ref[slot, e]
                acc = acc_ref[h].reshape(HEAD_DIM // sub, sub, t) * al[None]
                acc_ref[h] = acc.reshape(HEAD_DIM, t) + c[0:HEAD_DIM]
                l_ref[h] = al * l_ref[h] + c[HEAD_DIM:HEAD_DIM + sub]

        scores(0, 0)
        for k in range(n_pairs):
            if k + 1 < n_pairs:
                scores(k + 1, (k + 1) % 2)
            softmax(k, k % 2)
            values(k, k % 2)

    @pl.when(j < i)
    def _():
        step(False)

    @pl.when(j == i)
    def _():
        step(True)
        for h in range(N_HEADS):
            o = acc_ref[h].reshape(HEAD_DIM // sub, sub, t) / l_ref[h][None]
            oT_ref[h * HEAD_DIM:(h + 1) * HEAD_DIM, :] = o.reshape(HEAD_DIM, t)
        o_ref[...] = oT_ref[...].T.astype(BF16)


def _cast_rows(rows, n_steps):
    return next(r for r in range(16, rows + 1, 16) if rows % r == 0 and rows // r <= n_steps)


def _attention(k, ka, qT, vTb, cumT, w1, w2, *, batch, seq):
    n_pairs, m, _ = k.shape
    d = N_HEADS * HEAD_DIM
    t = ATTN_T
    nq = seq // t
    pairs = [(i, j) for i in range(nq) for j in range(i + 1)]
    qi = jnp.array([i for i, _ in pairs], jnp.int32)
    kj = jnp.array([j for _, j in pairs], jnp.int32)
    n_steps = len(pairs)
    kvrow = lambda b, s, qi, kj: (0, b * nq + kj[s], 0)

    def cast_spec(w):
        rows = _cast_rows(w.shape[0], batch * n_steps)
        last = w.shape[0] // rows - 1
        return pl.BlockSpec((rows, w.shape[1]), lambda b, s, qi, kj: (jnp.minimum(b * n_steps + s, last), 0))

    grid_spec = pltpu.PrefetchScalarGridSpec(
        num_scalar_prefetch=2,
        grid=(batch, n_steps),
        in_specs=[
            pl.BlockSpec((n_pairs, t, LANES), kvrow),
            pl.BlockSpec((n_pairs, t, LANES), kvrow),
            pl.BlockSpec((None, n_pairs, LANES, t), lambda b, s, qi, kj: (b, 0, 0, qi[s])),
            pl.BlockSpec((None, N_HEADS, HEAD_DIM, t), lambda b, s, qi, kj: (b, 0, 0, kj[s])),
            pl.BlockSpec((None, N_HEADS, t), lambda b, s, qi, kj: (b, 0, qi[s])),
            cast_spec(w1),
            cast_spec(w2),
        ],
        out_specs=[
            pl.BlockSpec((t, d), lambda b, s, qi, kj: (b * nq + qi[s], 0)),
            cast_spec(w1),
            cast_spec(w2),
        ],
        scratch_shapes=[
            pltpu.VMEM((N_HEADS, 2 * LANES, t), BF16),
            pltpu.VMEM((N_HEADS, 8, t), F32),
            pltpu.VMEM((N_HEADS, 8, t), F32),
            pltpu.VMEM((2, 2, 8, t), F32),
            pltpu.VMEM((N_HEADS, HEAD_DIM, t), F32),
            pltpu.VMEM((2, 2, t, t), F32),
            pltpu.VMEM((2, 2, t, t), BF16),
            pltpu.VMEM((d, t), F32),
        ],
    )
    return pl.pallas_call(
        functools.partial(_attn_kernel, t=t),
        grid_spec=grid_spec,
        out_shape=[jax.ShapeDtypeStruct((m, d), BF16),
                   jax.ShapeDtypeStruct(w1.shape, BF16), jax.ShapeDtypeStruct(w2.shape, BF16)],
        compiler_params=_cparams(("arbitrary", "arbitrary")),
        name="attention",
    )(qi, kj, k, ka, qT.reshape(batch, n_pairs, LANES, seq), vTb.reshape(batch, N_HEADS, HEAD_DIM, seq),
      cumT, w1, w2)


def _decode_init(b, n_req, qT_ref, kTs_ref, vTs_ref, lfTs_ref, state):
    qb_ref, acc_ref, m_ref, l_ref, carry_ref = state[:5]
    sel = lax.broadcasted_iota(jnp.int32, (1, n_req), 1) == b
    pick = lambda ref: jnp.sum(jnp.where(sel, ref[...], 0.0), axis=1, keepdims=True)
    qcol = pick(qT_ref)
    kcol = pick(kTs_ref)
    vcol = pick(vTs_ref)
    qk = qcol * kcol
    lane0 = lax.broadcasted_iota(jnp.int32, (1, LANES), 1) == 0
    for h in range(N_HEADS):
        hs = slice(h * HEAD_DIM, (h + 1) * HEAD_DIM)
        qb_ref[h] = jnp.broadcast_to(qcol[hs], (HEAD_DIM, LANES))
        acc_ref[h] = jnp.where(lane0, vcol[hs], 0.0)
        m_ref[h:h + 1, :] = jnp.sum(qk[hs], axis=0, keepdims=True)
    l_ref[...] = jnp.ones(l_ref.shape, F32)
    carry_ref[...] = pick(lfTs_ref)


def _decode_scores(k_refs, state):
    qb_ref, s_ref = state[0], state[5]
    for h in range(N_HEADS):
        qb = qb_ref[h]
        for g, k_ref in enumerate(k_refs):
            s_ref[g, h:h + 1, :] = jnp.sum(k_ref[h] * qb, axis=0, keepdims=True)


def _decode_probs(lf_refs, state):
    _, _, m_ref, l_ref, carry_ref, s_ref, p_ref, ab_ref = state
    r = lax.broadcasted_iota(jnp.int32, (PAGE_SIZE, PAGE_SIZE), 0)
    c = lax.broadcasted_iota(jnp.int32, (PAGE_SIZE, PAGE_SIZE), 1)
    after = (r > c).astype(BF16)
    carry = carry_ref[...]
    logits = []
    for g, lf_ref in enumerate(lf_refs):
        lfT = lf_ref[...]
        logits.append(s_ref[g] + (_dot3(lfT, after) + carry))
        carry = carry + jnp.sum(lfT, axis=1, keepdims=True)
    carry_ref[...] = carry
    m_prev = m_ref[...]
    m_next = jnp.maximum(m_prev, jnp.max(functools.reduce(jnp.maximum, logits), axis=1, keepdims=True))
    alpha = jnp.exp(m_prev - m_next)
    psum = None
    for g in range(len(lf_refs)):
        p = jnp.exp(logits[g] - m_next)
        p_ref[g] = p
        psum = p if psum is None else psum + p
    l_ref[...] = alpha * l_ref[...] + jnp.sum(psum, axis=1, keepdims=True)
    m_ref[...] = m_next
    ab_ref[...] = jnp.broadcast_to(alpha, ab_ref.shape)


def _decode_values(v_refs, state):
    acc_ref, p_ref, ab_ref = state[1], state[6], state[7]
    for h in range(N_HEADS):
        acc = ab_ref[h:h + 1, :] * acc_ref[h]
        for g, v_ref in enumerate(v_refs):
            acc = acc + v_ref[h] * p_ref[g, h:h + 1, :]
        acc_ref[h] = acc


def _decode_finish(o_ref, state):
    acc_ref, l_ref = state[1], state[3]
    for h in range(N_HEADS):
        acc_ref[h] = acc_ref[h] / l_ref[h:h + 1, :]
    ones = jnp.ones((8, LANES), BF16)
    o8 = _nt_dot3(ones, acc_ref[...].reshape(N_HEADS * HEAD_DIM, LANES))
    o_ref[...] = o8[0:1, :].astype(BF16)


def _decode_scratch(group):
    return [
        pltpu.VMEM((N_HEADS, HEAD_DIM, LANES), F32),
        pltpu.VMEM((N_HEADS, HEAD_DIM, LANES), F32),
        pltpu.VMEM((N_HEADS, 1), F32),
        pltpu.VMEM((N_HEADS, 1), F32),
        pltpu.VMEM((N_HEADS, 1), F32),
        pltpu.VMEM((group, N_HEADS, PAGE_SIZE), F32),
        pltpu.VMEM((group, N_HEADS, PAGE_SIZE), F32),
        pltpu.VMEM((N_HEADS, LANES), F32),
    ]


def _decode_specs(n_req, group, page_of):
    d = N_HEADS * HEAD_DIM
    small = lambda *a: (0, 0)
    kv_block = (None, N_HEADS, HEAD_DIM, PAGE_SIZE)
    kv_map = lambda g: (lambda *a: (page_of(g, *a), 0, 0, 0))
    lf_map = lambda g: (lambda *a: (page_of(g, *a), 0, 0))
    specs = [pl.BlockSpec((d, n_req), small)] * 3 + [pl.BlockSpec((N_HEADS, n_req), small)]
    specs += [pl.BlockSpec(kv_block, kv_map(g)) for g in range(group)]
    specs += [pl.BlockSpec(kv_block, kv_map(g)) for g in range(group)]
    specs += [pl.BlockSpec((None, N_HEADS, PAGE_SIZE), lf_map(g)) for g in range(group)]
    return specs


def _ffn_kernel(*refs, n_chunks, final_norm, dec):
    if dec is None:
        x_ref, z_ref, wmo_ref, g_ref, wg_ref, wu_ref, wo_ref, gf_ref, out_ref, acc_ref, h_ref = refs
    else:
        n_req, group, spr = dec
        x_ref, z_ref, wmo_ref, g_ref, wg_ref, wu_ref, wo_ref, gf_ref = refs[1:9]
        qT_ref, kTs_ref, vTs_ref, lfTs_ref = refs[9:13]
        k_refs = refs[13:13 + group]
        v_refs = refs[13 + group:13 + 2 * group]
        lf_refs = refs[13 + 2 * group:13 + 3 * group]
        out_ref, o_dec_ref, acc_ref, h_ref = refs[13 + 3 * group:17 + 3 * group]
        state = refs[17 + 3 * group:]
        step = pl.program_id(0) * n_chunks + pl.program_id(1)
    c = pl.program_id(1)

    @pl.when(c == 0)
    def _():
        y1 = x_ref[...] + _dot(z_ref[...], wmo_ref[...])
        acc_ref[...] = y1
        h_ref[...] = _rms(y1, g_ref[...]).astype(BF16)

    if dec is not None:
        active = step < n_req * spr

        @pl.when((step % spr == 0) & active)
        def _():
            _decode_init(step // spr, n_req, qT_ref, kTs_ref, vTs_ref, lfTs_ref, state)

    def chunk(with_pages):
        if with_pages:
            _decode_scores(k_refs, state)
            _decode_probs(lf_refs, state)
            _decode_values(v_refs, state)
        h = h_ref[...]
        gate = _dot(h, wg_ref[...])
        up = _dot(h, wu_ref[...])
        a = (gate * jax.nn.sigmoid(gate) * up).astype(BF16)
        acc_ref[...] += _dot(a, wo_ref[...])

    if dec is None:
        chunk(False)
    else:
        pl.when(active)(lambda: chunk(True))
        pl.when(jnp.logical_not(active))(lambda: chunk(False))

    @pl.when(c == n_chunks - 1)
    def _():
        y = acc_ref[...]
        out_ref[...] = _rms(y, gf_ref[...]) if final_norm else y

    if dec is not None:
        @pl.when((step % spr == spr - 1) & active)
        def _():
            _decode_finish(o_dec_ref, state)


def _mix_ffn(x, z, wmo, g, w_in, w_out, g_final, *, layer, tm, tc, final_norm, decode=None):
    m, d = x.shape
    d_ff = w_out.shape[1]
    n_chunks = d_ff // tc
    row = lambda i, c, *pt: (i, 0)
    const = lambda i, c, *pt: (0, 0)
    in_specs = [
        pl.BlockSpec((tm, d), row),
        pl.BlockSpec((tm, d), row),
        pl.BlockSpec((d, d), const),
        pl.BlockSpec((1, d), const),
        pl.BlockSpec((None, d, tc), lambda i, c, *pt: (layer, 0, c)),
        pl.BlockSpec((None, d, tc), lambda i, c, *pt: (layer, 0, n_chunks + c)),
        pl.BlockSpec((None, tc, d), lambda i, c, *pt: (layer, c, 0)),
        pl.BlockSpec((1, d), const),
    ]
    scratch = [pltpu.VMEM((tm, d), F32), pltpu.VMEM((tm, d), BF16)]
    operands = (x, z, wmo, g, w_in, w_in, w_out, g_final)
    grid = (m // tm, n_chunks)
    y_spec = pl.BlockSpec((tm, d), row)
    y_shape = jax.ShapeDtypeStruct((m, d), F32)
    if decode is None:
        return pl.pallas_call(
            functools.partial(_ffn_kernel, n_chunks=n_chunks, final_norm=final_norm, dec=None),
            grid=grid, in_specs=in_specs, out_specs=y_spec, out_shape=y_shape,
            scratch_shapes=scratch,
            compiler_params=_cparams(("arbitrary", "arbitrary")),
            name="mix_ffn",
        )(*operands)

    page_table, qT, kTs, vTs, lfTs, kT_cache, vT_cache, lfT_cache = decode
    n_req, n_pages = page_table.shape
    group = FFN_DEC_G
    spr = n_pages // group
    assert grid[0] * grid[1] >= n_req * spr, "not enough grid steps for these requests"

    def request_of(i, c):
        step = i * n_chunks + c
        valid = step < n_req * spr
        return jnp.where(valid, step // spr, n_req - 1), jnp.where(valid, step % spr, spr - 1)

    def page_of(g, i, c, pt):
        req, part = request_of(i, c)
        return pt[req, n_pages - 1 - (part * group + g)]

    grid_spec = pltpu.PrefetchScalarGridSpec(
        num_scalar_prefetch=1,
        grid=grid,
        in_specs=in_specs + _decode_specs(n_req, group, page_of),
        out_specs=[y_spec, pl.BlockSpec((None, 1, d), lambda i, c, pt: (request_of(i, c)[0], 0, 0))],
        scratch_shapes=scratch + _decode_scratch(group),
    )
    y, o_dec = pl.pallas_call(
        functools.partial(_ffn_kernel, n_chunks=n_chunks, final_norm=final_norm, dec=(n_req, group, spr)),
        grid_spec=grid_spec,
        out_shape=[y_shape, jax.ShapeDtypeStruct((n_req, 1, d), BF16)],
        compiler_params=_cparams(("arbitrary", "arbitrary")),
        name="mix_ffn_decode",
    )(page_table, *operands, qT, kTs, vTs, lfTs,
      *([kT_cache] * group), *([vT_cache] * group), *([lfT_cache] * group))
    return y, o_dec.reshape(n_req, d)


def _conv_prompt_kernel(x_ref, g_ref, wb_ref, wc_ref, wu_ref, ck_ref, prev_ref, z_ref, st_ref,
                        buf_ref, *, tm, blocks_per_seq):
    i = pl.program_id(0)
    h = _rms(x_ref[...], g_ref[...]).astype(BF16)
    gate = _dot(h, wb_ref[...])
    cu = _dot(h, wc_ref[...]) * _dot(h, wu_ref[...])

    @pl.when(i % blocks_per_seq == 0)
    def _():
        buf_ref[6:8, :] = prev_ref[...]

    buf_ref[8:8 + tm, :] = cu
    y = ck_ref[0:1, :] * buf_ref[6:6 + tm, :] + ck_ref[1:2, :] * buf_ref[7:7 + tm, :]
    y = y + ck_ref[2:3, :] * cu
    z_ref[...] = (gate * y).astype(BF16)
    tail = cu[tm - 2:tm, :]
    st_ref[...] = tail
    buf_ref[6:8, :] = tail


def _conv_prompt(x, g, w_in, ck, prev, *, batch, seq):
    m, d = x.shape
    tm = CONV_TM
    bps = seq // tm
    const = lambda i: (0, 0)
    return pl.pallas_call(
        functools.partial(_conv_prompt_kernel, tm=tm, blocks_per_seq=bps),
        grid=(m // tm,),
        in_specs=[
            pl.BlockSpec((tm, d), lambda i: (i, 0)),
            pl.BlockSpec((1, d), const),
            pl.BlockSpec((d, d), lambda i: (0, 0)),
            pl.BlockSpec((d, d), lambda i: (0, 1)),
            pl.BlockSpec((d, d), lambda i: (0, 2)),
            pl.BlockSpec((3, d), const),
            pl.BlockSpec((None, 2, d), lambda i: (i // bps, 0, 0)),
        ],
        out_specs=[
            pl.BlockSpec((tm, d), lambda i: (i, 0)),
            pl.BlockSpec((None, 2, d), lambda i: (i // bps, 0, 0)),
        ],
        out_shape=[
            jax.ShapeDtypeStruct((m, d), BF16),
            jax.ShapeDtypeStruct((batch, 2, d), F32),
        ],
        scratch_shapes=[pltpu.VMEM((tm + 8, d), F32)],
        compiler_params=_cparams(("arbitrary",)),
        name="conv_prompt",
    )(x, g, w_in, w_in, w_in, ck, prev)


def _conv_sample_kernel(x_ref, g_ref, wb_ref, wc_ref, wu_ref, ck_ref, p0_ref, p1_ref, z_ref, cu_ref):
    h = _rms(x_ref[...], g_ref[...]).astype(BF16)
    gate = _dot(h, wb_ref[...])
    cu = _dot(h, wc_ref[...]) * _dot(h, wu_ref[...])
    y = ck_ref[0:1, :] * p0_ref[...] + ck_ref[1:2, :] * p1_ref[...]
    y = y + ck_ref[2:3, :] * cu
    z_ref[...] = (gate * y).astype(BF16)
    cu_ref[...] = cu


def _conv_sample(x, g, w_in, ck, p0, p1):
    n, d = x.shape
    full = lambda shape: pl.BlockSpec(shape, lambda i: (0, 0))
    return pl.pallas_call(
        _conv_sample_kernel,
        grid=(1,),
        in_specs=[
            full((n, d)),
            full((1, d)),
            pl.BlockSpec((d, d), lambda i: (0, 0)),
            pl.BlockSpec((d, d), lambda i: (0, 1)),
            pl.BlockSpec((d, d), lambda i: (0, 2)),
            full((3, d)),
            full((n, d)),
            full((n, d)),
        ],
        out_specs=[full((n, d)), full((n, d))],
        out_shape=[jax.ShapeDtypeStruct((n, d), BF16), jax.ShapeDtypeStruct((n, d), F32)],
        compiler_params=_cparams(("arbitrary",)),
        name="conv_sample",
    )(x, g, w_in, w_in, w_in, ck, p0, p1)


def kernel(x_prompt, x_sample, cache_k, cache_v, cache_logf, state_conv, page_table,
           norm_mix, norm_ffn, w_ffn_in, w_ffn_out, norm_final,
           attn_w_qkvf, attn_b_f, attn_w_o, conv_w_in, conv_kernel, conv_w_out):
    batch, seq, d = x_prompt.shape
    n_req = x_sample.shape[0]
    assert x_sample.shape[1] == 1 and d == N_HEADS * HEAD_DIM
    assert norm_mix.shape[0] == 2 and cache_k.shape[2] == PAGE_SIZE

    xp = x_prompt.reshape(batch * seq, d)
    xs = x_sample.reshape(n_req, d)
    gfin = norm_final.reshape(1, d)

    wT = attn_w_qkvf[0].T.astype(BF16)
    bf = attn_b_f[0].reshape(N_HEADS, 1)
    g0 = norm_mix[0].reshape(1, d)

    qT, k, kT, vT, vTb, lfT = _proj_prompt(xp, g0, wT, bf, batch=batch, seq=seq)
    cumT, ka = _cumsum(lfT)
    n_layers, _, d_in2 = w_ffn_in.shape
    d_ff = w_ffn_out.shape[1]
    o_p, w_in, w_out = _attention(k, ka, qT, vTb, cumT, w_ffn_in.reshape(n_layers * d, d_in2),
                                  w_ffn_out.reshape(n_layers * d_ff, d), batch=batch, seq=seq)
    w_in = w_in.reshape(w_ffn_in.shape)
    w_out = w_out.reshape(w_ffn_out.shape)

    qTs, kTs, vTs, lfTs = _proj_sample(xs, g0, wT, bf)
    caches = (jnp.transpose(cache_k[0], (0, 2, 3, 1)),
              jnp.transpose(cache_v[0], (0, 2, 3, 1)),
              jnp.transpose(cache_logf[0], (0, 2, 1)))
    n_pages = page_table.shape[1]
    ffn_steps = (batch * seq // FFN_TM) * (w_ffn_out.shape[1] // FFN_TC)
    capacity = ffn_steps // (n_pages // FFN_DEC_G)
    n_ride = (n_req + 1) // 2
    assert n_ride <= capacity

    def requests(lo, hi):
        return (page_table[lo:hi], qTs[:, lo:hi], kTs[:, lo:hi], vTs[:, lo:hi], lfTs[:, lo:hi], *caches)

    wo = attn_w_o[0].astype(BF16)
    gf0 = norm_ffn[0].reshape(1, d)
    yp, o_s0 = _mix_ffn(xp, o_p, wo, gf0, w_in, w_out, gfin, layer=0, tm=FFN_TM, tc=FFN_TC,
                        final_norm=False, decode=requests(0, n_ride))

    wci = conv_w_in[0].astype(BF16)
    g1 = norm_mix[1].reshape(1, d)
    ck = conv_kernel[0]
    wco = conv_w_out[0].astype(BF16)
    gf1 = norm_ffn[1].reshape(1, d)
    zp, conv_p = _conv_prompt(yp, g1, wci, ck, jnp.zeros((batch, 2, d), F32), batch=batch, seq=seq)
    yp, o_s1 = _mix_ffn(yp, zp, wco, gf1, w_in, w_out, gfin, layer=1, tm=FFN_TM, tc=FFN_TC,
                        final_norm=True, decode=requests(n_ride, n_req))

    o_s = jnp.concatenate([o_s0, o_s1], axis=0)
    tc_s = w_ffn_out.shape[1] // 2
    ys = _mix_ffn(xs, o_s, wo, gf0, w_in, w_out, gfin, layer=0, tm=n_req, tc=tc_s, final_norm=False)
    zs, cu_s = _conv_sample(ys, g1, wci, ck, state_conv[0, :, 0], state_conv[0, :, 1])
    ys = _mix_ffn(ys, zs, wco, gf1, w_in, w_out, gfin, layer=1, tm=n_req, tc=tc_s, final_norm=True)

    to_heads = lambda t: jnp.transpose(t.reshape(batch, N_HEADS, HEAD_DIM, seq), (0, 3, 1, 2))[None]
    to_heads_s = lambda t: t.T.reshape(1, n_req, 1, N_HEADS, HEAD_DIM)
    return (
        yp.reshape(batch, seq, d),
        ys.reshape(n_req, 1, d),
        to_heads(kT),
        to_heads(vT),
        jnp.transpose(lfT, (0, 2, 1))[None],
        to_heads_s(kTs),
        to_heads_s(vTs),
        lfTs.T.reshape(1, n_req, 1, N_HEADS),
        conv_p[None],
        jnp.stack([state_conv[0, :, 1], cu_s], axis=1)[None],
    )
```

```python
import functools

import jax
import jax.numpy as jnp
from jax import lax
from jax.experimental import pallas as pl
from jax.experimental.pallas import tpu as pltpu

F32 = jnp.float32
BF16 = jnp.bfloat16

RMS_EPS = 1e-6
NEG_INF = -1e30
N_HEADS = 16
HEAD_DIM = 64
PAGE_SIZE = 128
LANES = 128
VMEM_LIMIT = 56 * 1024 * 1024

PROJ_TM = 1024
ATTN_T = 512
ATTN_CH = 64
LOG2E = 1.4426950408889634
FFN_TM = 1024
FFN_TC = 256
CONV_TM = 1024
FFN_DEC_G = 8


def _cparams(sem):
    return pltpu.CompilerParams(dimension_semantics=sem, vmem_limit_bytes=VMEM_LIMIT)


def _rms(x, g):
    ms = jnp.mean(x * x, axis=-1, keepdims=True)
    return x * lax.rsqrt(ms + RMS_EPS) * g


def _dot(a, b):
    return jnp.dot(a, b, preferred_element_type=F32)


def _nt_dot(a, b):
    return lax.dot_general(a, b, (((1,), (1,)), ((), ())), preferred_element_type=F32)


def _log_sigmoid(x):
    return jnp.minimum(x, 0.0) - jnp.log1p(jnp.exp(-jnp.abs(x)))


def _split3(x):
    hi = x.astype(BF16)
    r1 = x - hi.astype(F32)
    mid = r1.astype(BF16)
    lo = (r1 - mid.astype(F32)).astype(BF16)
    return hi, mid, lo


def _dot3(x, w):
    hi, mid, lo = _split3(x)
    return _dot(hi, w) + _dot(mid, w) + _dot(lo, w)


def _nt_dot3(w, x):
    hi, mid, lo = _split3(x)
    return _nt_dot(w, hi) + _nt_dot(w, mid) + _nt_dot(w, lo)


def _qkvf_specs(d, index):
    once = pl.Buffered(1)
    return [
        pl.BlockSpec((d, d), index(0), pipeline_mode=once),
        pl.BlockSpec((d, d), index(1), pipeline_mode=once),
        pl.BlockSpec((d, d), index(2), pipeline_mode=once),
        pl.BlockSpec((N_HEADS, d), index(3 * d // N_HEADS), pipeline_mode=once),
    ]


def _proj_prompt_kernel(x_ref, g_ref, wqT_ref, wkT_ref, wvT_ref, wfT_ref, bf_ref,
                        qT_ref, k_ref, kT_ref, vT_ref, vTb_ref, lfT_ref, *, scale):
    h = _rms(x_ref[...], g_ref[...]).astype(BF16)
    qT_ref[...] = (_nt_dot(wqT_ref[...], h) * scale).astype(BF16)
    kT = _nt_dot(wkT_ref[...], h)
    kT_ref[...] = kT
    k = kT.T.astype(BF16)
    for hp in range(N_HEADS // 2):
        k_ref[hp] = k[:, hp * LANES:(hp + 1) * LANES]
    vT = _nt_dot(wvT_ref[...], h)
    vT_ref[...] = vT
    vTb_ref[...] = vT.astype(BF16)
    lfT_ref[...] = _log_sigmoid(_nt_dot(wfT_ref[...], h) + bf_ref[...])


def _proj_prompt(x, g, wT, bf, *, batch, seq):
    m, d = x.shape
    tm = PROJ_TM
    nb = seq // tm
    n_pairs = N_HEADS // 2
    const = lambda i: (0, 0)
    tmaj = lambda i: (i // nb, 0, i % nb)
    return pl.pallas_call(
        functools.partial(_proj_prompt_kernel, scale=HEAD_DIM ** -0.5 * LOG2E),
        grid=(m // tm,),
        in_specs=[
            pl.BlockSpec((tm, d), lambda i: (i, 0)),
            pl.BlockSpec((1, d), const),
            *_qkvf_specs(d, lambda k: (lambda i: (k, 0))),
            pl.BlockSpec((N_HEADS, 1), const),
        ],
        out_specs=[
            pl.BlockSpec((None, d, tm), tmaj),
            pl.BlockSpec((n_pairs, tm, LANES), lambda i: (0, i, 0)),
            pl.BlockSpec((None, d, tm), tmaj),
            pl.BlockSpec((None, d, tm), tmaj),
            pl.BlockSpec((None, d, tm), tmaj),
            pl.BlockSpec((None, N_HEADS, tm), tmaj),
        ],
        out_shape=[
            jax.ShapeDtypeStruct((batch, d, seq), BF16),
            jax.ShapeDtypeStruct((n_pairs, m, LANES), BF16),
            jax.ShapeDtypeStruct((batch, d, seq), F32),
            jax.ShapeDtypeStruct((batch, d, seq), F32),
            jax.ShapeDtypeStruct((batch, d, seq), BF16),
            jax.ShapeDtypeStruct((batch, N_HEADS, seq), F32),
        ],
        compiler_params=_cparams(("arbitrary",)),
        name="proj_prompt",
    )(x, g, wT, wT, wT, wT, bf)


def _proj_sample_kernel(x_ref, g_ref, wqT_ref, wkT_ref, wvT_ref, wfT_ref, bf_ref,
                        qT_ref, kT_ref, vT_ref, lfT_ref, *, scale):
    h = _rms(x_ref[...], g_ref[...]).astype(BF16)
    qT_ref[...] = _nt_dot(wqT_ref[...], h) * scale
    kT_ref[...] = _nt_dot(wkT_ref[...], h)
    vT_ref[...] = _nt_dot(wvT_ref[...], h)
    lfT_ref[...] = _log_sigmoid(_nt_dot(wfT_ref[...], h) + bf_ref[...])


def _proj_sample(x, g, wT, bf):
    n, d = x.shape
    full = lambda shape: pl.BlockSpec(shape, lambda i: (0, 0))
    return pl.pallas_call(
        functools.partial(_proj_sample_kernel, scale=HEAD_DIM ** -0.5),
        grid=(1,),
        in_specs=[
            full((n, d)),
            full((1, d)),
            *_qkvf_specs(d, lambda k: (lambda i: (k, 0))),
            full((N_HEADS, 1)),
        ],
        out_specs=[full((d, n)), full((d, n)), full((d, n)), full((N_HEADS, n))],
        out_shape=[
            jax.ShapeDtypeStruct((d, n), F32),
            jax.ShapeDtypeStruct((d, n), F32),
            jax.ShapeDtypeStruct((d, n), F32),
            jax.ShapeDtypeStruct((N_HEADS, n), F32),
        ],
        compiler_params=_cparams(("arbitrary",)),
        name="proj_sample",
    )(x, g, wT, wT, wT, wT, bf)


def _cumsum_kernel(lfT_ref, cumT_ref, ka_ref, *, seq):
    blk = LANES
    n_pairs = N_HEADS // 2
    r = lax.broadcasted_iota(jnp.int32, (blk, blk), 0)
    c = lax.broadcasted_iota(jnp.int32, (blk, blk), 1)
    upper = (r <= c).astype(BF16)
    eye = (r == c).astype(BF16)
    slot = LANES // n_pairs
    head = lax.broadcasted_iota(jnp.int32, (N_HEADS, LANES), 0)
    col = lax.broadcasted_iota(jnp.int32, (N_HEADS, LANES), 1)
    target = (head >> 1) * slot + 3 + 3 * (head & 1)
    place = [(col == target + part).astype(BF16) for part in range(3)]
    lane = lax.broadcasted_iota(jnp.int32, (1, LANES), 1)
    ones_lanes = ((lane & (slot - 1)) < 3).astype(F32)
    carry = jnp.zeros((N_HEADS, 1), F32)
    for i in range(seq // blk):
        sl = slice(i * blk, (i + 1) * blk)
        cb = _dot3(lfT_ref[:, sl], upper) + carry
        cumT_ref[:, sl] = cb
        carry = cb[:, blk - 1:blk]
        cbt = _nt_dot3(eye, cb)
        hi, mid, lo = _split3(-(cbt * LOG2E))
        packed = _dot(hi, place[0]) + _dot(mid, place[1]) + _dot(lo, place[2]) + ones_lanes
        for hp in range(n_pairs):
            moved = packed if hp == 0 else pltpu.roll(packed, LANES - slot * hp, axis=1)
            ka_ref[hp, sl, :] = jnp.where(lane < slot, moved, 0.0).astype(BF16)


def _cumsum(lfT):
    batch, nh, seq = lfT.shape
    n_pairs = nh // 2
    return pl.pallas_call(
        functools.partial(_cumsum_kernel, seq=seq),
        grid=(batch,),
        in_specs=[pl.BlockSpec((None, nh, seq), lambda b: (b, 0, 0))],
        out_specs=[
            pl.BlockSpec((None, nh, seq), lambda b: (b, 0, 0)),
            pl.BlockSpec((n_pairs, seq, LANES), lambda b: (0, b, 0)),
        ],
        out_shape=[
            jax.ShapeDtypeStruct((batch, nh, seq), F32),
            jax.ShapeDtypeStruct((n_pairs, batch * seq, LANES), BF16),
        ],
        compiler_params=_cparams(("arbitrary",)),
        name="cumsum",
    )(lfT)


def _attn_kernel(qi_ref, kj_ref, k_ref, ka_ref, qT_ref, vT_ref, cumT_ref, w1_ref, w2_ref,
                 o_ref, w1b_ref, w2b_ref,
                 qop_ref, m_ref, l_ref, al_ref, acc_ref, s_ref, p_ref, oT_ref, *, t):
    w1b_ref[...] = w1_ref[...].astype(BF16)
    w2b_ref[...] = w2_ref[...].astype(BF16)
    first = (pl.program_id(0) == 0) & (pl.program_id(1) == 0)
    i = qi_ref[pl.program_id(1)]
    j = kj_ref[pl.program_id(1)]
    n_pairs = N_HEADS // 2
    ch = ATTN_CH
    sub = 8

    @pl.when(first)
    def _():
        qop_ref[...] = jnp.zeros(qop_ref.shape, BF16)

    @pl.when(j == 0)
    def _():
        m_ref[...] = jnp.full(m_ref.shape, NEG_INF, F32)
        l_ref[...] = jnp.zeros(l_ref.shape, F32)
        acc_ref[...] = jnp.zeros(acc_ref.shape, F32)
        parts = [x.astype(F32) for x in _split3(cumT_ref[...] * LOG2E)]
        row16 = lax.broadcasted_iota(jnp.int32, (16, t), 0)
        for h in range(N_HEADS):
            e = h % 2
            own = slice(e * HEAD_DIM, (e + 1) * HEAD_DIM)
            qop_ref[h, own, :] = qT_ref[h // 2, own, :]
            hi, mid, lo = (x[h:h + 1, :] for x in parts)
            ones = jnp.where((row16 >= 3 + 3 * e) & (row16 < 6 + 3 * e), 1.0, 0.0)
            ext = jnp.where(row16 == 0, hi, jnp.where(row16 == 1, mid, jnp.where(row16 == 2, lo, ones)))
            qop_ref[h, LANES:LANES + 16, :] = ext.astype(BF16)

    def step(diagonal):
        if diagonal:
            row_minus_col = (lax.broadcasted_iota(jnp.int32, (ch, LANES), 0)
                             - lax.broadcasted_iota(jnp.int32, (ch, LANES), 1))
        n_ch, n_u = t // ch, t // LANES
        ones_rows = jnp.ones((16, t), BF16)

        def above(r, u):
            return diagonal and r * ch > u * LANES + LANES - 1

        def crosses(r, u):
            return diagonal and not above(r, u) and r * ch + ch - 1 > u * LANES

        def scores(hp, slot):
            kx = jnp.concatenate([k_ref[hp], ka_ref[hp]], axis=1)
            for e in range(2):
                s_ref[slot, e] = _dot(kx, qop_ref[2 * hp + e])

        def softmax(hp, slot):
            for e in range(2):
                h = 2 * hp + e
                for u in range(n_u):
                    lsl = slice(u * LANES, (u + 1) * LANES)
                    mx = None
                    for r in range(n_ch):
                        if above(r, u):
                            continue
                        rows = slice(r * ch, (r + 1) * ch)
                        blk = s_ref[slot, e, rows, lsl]
                        if crosses(r, u):
                            blk = jnp.where(row_minus_col <= (u * LANES - r * ch), blk, NEG_INF)
                            s_ref[slot, e, rows, lsl] = blk
                        part = jnp.max(blk.reshape(ch // sub, sub, LANES), axis=0)
                        mx = part if mx is None else jnp.maximum(mx, part)
                    m_old = m_ref[h, :, lsl]
                    m_new = jnp.maximum(m_old, jnp.max(mx, axis=0, keepdims=True))
                    al_ref[slot, e, :, lsl] = jnp.exp2(m_old - m_new)
                    m_ref[h, :, lsl] = m_new
                    for r in range(n_ch):
                        rows = slice(r * ch, (r + 1) * ch)
                        if above(r, u):
                            p_ref[slot, e, rows, lsl] = jnp.zeros((ch, LANES), BF16)
                            continue
                        blk = s_ref[slot, e, rows, lsl].reshape(ch // sub, sub, LANES)
                        p = jnp.exp2(blk - m_new[None])
                        p_ref[slot, e, rows, lsl] = p.reshape(ch, LANES).astype(BF16)

        def values(hp, slot):
            for e in range(2):
                h = 2 * hp + e
                vx = jnp.concatenate([vT_ref[h], ones_rows], axis=0)
                c = _dot(vx, p_ref[slot, e])
                al = al_ref[slot, e]
                acc = acc_ref[h].reshape(HEAD_DIM // sub, sub, t) * al[None]
                acc_ref[h] = acc.reshape(HEAD_DIM, t) + c[0:HEAD_DIM]
                l_ref[h] = al * l_ref[h] + c[HEAD_DIM:HEAD_DIM + sub]

        scores(0, 0)
        for k in range(n_pairs):
            if k + 1 < n_pairs:
                scores(k + 1, (k + 1) % 2)
            softmax(k, k % 2)
            values(k, k % 2)

    @pl.when(j < i)
    def _():
        step(False)

    @pl.when(j == i)
    def _():
        step(True)
        for h in range(N_HEADS):
            o = acc_ref[h].reshape(HEAD_DIM // sub, sub, t) / l_ref[h][None]
            oT_ref[h * HEAD_DIM:(h + 1) * HEAD_DIM, :] = o.reshape(HEAD_DIM, t)
        o_ref[...] = oT_ref[...].T.astype(BF16)


def _cast_rows(rows, n_steps):
    return next(r for r in range(16, rows + 1, 16) if rows % r == 0 and rows // r <= n_steps)


def _attention(k, ka, qT, vTb, cumT, w1, w2, *, batch, seq):
    n_pairs, m, _ = k.shape
    d = N_HEADS * HEAD_DIM
    t = ATTN_T
    nq = seq // t
    pairs = [(i, j) for i in range(nq) for j in range(i + 1)]
    qi = jnp.array([i for i, _ in pairs], jnp.int32)
    kj = jnp.array([j for _, j in pairs], jnp.int32)
    n_steps = len(pairs)
    kvrow = lambda b, s, qi, kj: (0, b * nq + kj[s], 0)

    def cast_spec(w):
        rows = _cast_rows(w.shape[0], batch * n_steps)
        last = w.shape[0] // rows - 1
        return pl.BlockSpec((rows, w.shape[1]), lambda b, s, qi, kj: (jnp.minimum(b * n_steps + s, last), 0))

    grid_spec = pltpu.PrefetchScalarGridSpec(
        num_scalar_prefetch=2,
        grid=(batch, n_steps),
        in_specs=[
            pl.BlockSpec((n_pairs, t, LANES), kvrow),
            pl.BlockSpec((n_pairs, t, LANES), kvrow),
            pl.BlockSpec((None, n_pairs, LANES, t), lambda b, s, qi, kj: (b, 0, 0, qi[s])),
            pl.BlockSpec((None, N_HEADS, HEAD_DIM, t), lambda b, s, qi, kj: (b, 0, 0, kj[s])),
            pl.BlockSpec((None, N_HEADS, t), lambda b, s, qi, kj: (b, 0, qi[s])),
            cast_spec(w1),
            cast_spec(w2),
        ],
        out_specs=[
            pl.BlockSpec((t, d), lambda b, s, qi, kj: (b * nq + qi[s], 0)),
            cast_spec(w1),
            cast_spec(w2),
        ],
        scratch_shapes=[
            pltpu.VMEM((N_HEADS, 2 * LANES, t), BF16),
            pltpu.VMEM((N_HEADS, 8, t), F32),
            pltpu.VMEM((N_HEADS, 8, t), F32),
            pltpu.VMEM((2, 2, 8, t), F32),
            pltpu.VMEM((N_HEADS, HEAD_DIM, t), F32),
            pltpu.VMEM((2, 2, t, t), F32),
            pltpu.VMEM((2, 2, t, t), BF16),
            pltpu.VMEM((d, t), F32),
        ],
    )
    return pl.pallas_call(
        functools.partial(_attn_kernel, t=t),
        grid_spec=grid_spec,
        out_shape=[jax.ShapeDtypeStruct((m, d), BF16),
                   jax.ShapeDtypeStruct(w1.shape, BF16), jax.ShapeDtypeStruct(w2.shape, BF16)],
        compiler_params=_cparams(("arbitrary", "arbitrary")),
        name="attention",
    )(qi, kj, k, ka, qT.reshape(batch, n_pairs, LANES, seq), vTb.reshape(batch, N_HEADS, HEAD_DIM, seq),
      cumT, w1, w2)


def _decode_init(b, n_req, qT_ref, kTs_ref, vTs_ref, lfTs_ref, state):
    qb_ref, acc_ref, m_ref, l_ref, carry_ref = state[:5]
    sel = lax.broadcasted_iota(jnp.int32, (1, n_req), 1) == b
    pick = lambda ref: jnp.sum(jnp.where(sel, ref[...], 0.0), axis=1, keepdims=True)
    qcol = pick(qT_ref)
    kcol = pick(kTs_ref)
    vcol = pick(vTs_ref)
    qk = qcol * kcol
    lane0 = lax.broadcasted_iota(jnp.int32, (1, LANES), 1) == 0
    for h in range(N_HEADS):
        hs = slice(h * HEAD_DIM, (h + 1) * HEAD_DIM)
        qb_ref[h] = jnp.broadcast_to(qcol[hs], (HEAD_DIM, LANES))
        acc_ref[h] = jnp.where(lane0, vcol[hs], 0.0)
        m_ref[h:h + 1, :] = jnp.sum(qk[hs], axis=0, keepdims=True)
    l_ref[...] = jnp.ones(l_ref.shape, F32)
    carry_ref[...] = pick(lfTs_ref)


def _decode_scores(k_refs, state):
    qb_ref, s_ref = state[0], state[5]
    for h in range(N_HEADS):
        qb = qb_ref[h]
        for g, k_ref in enumerate(k_refs):
            s_ref[g, h:h + 1, :] = jnp.sum(k_ref[h] * qb, axis=0, keepdims=True)


def _decode_probs(lf_refs, state):
    _, _, m_ref, l_ref, carry_ref, s_ref, p_ref, ab_ref = state
    r = lax.broadcasted_iota(jnp.int32, (PAGE_SIZE, PAGE_SIZE), 0)
    c = lax.broadcasted_iota(jnp.int32, (PAGE_SIZE, PAGE_SIZE), 1)
    after = (r > c).astype(BF16)
    carry = carry_ref[...]
    logits = []
    for g, lf_ref in enumerate(lf_refs):
        lfT = lf_ref[...]
        logits.append(s_ref[g] + (_dot3(lfT, after) + carry))
        carry = carry + jnp.sum(lfT, axis=1, keepdims=True)
    carry_ref[...] = carry
    m_prev = m_ref[...]
    m_next = jnp.maximum(m_prev, jnp.max(functools.reduce(jnp.maximum, logits), axis=1, keepdims=True))
    alpha = jnp.exp(m_prev - m_next)
    psum = None
    for g in range(len(lf_refs)):
        p = jnp.exp(logits[g] - m_next)
        p_ref[g] = p
        psum = p if psum is None else psum + p
    l_ref[...] = alpha * l_ref[...] + jnp.sum(psum, axis=1, keepdims=True)
    m_ref[...] = m_next
    ab_ref[...] = jnp.broadcast_to(alpha, ab_ref.shape)


def _decode_values(v_refs, state):
    acc_ref, p_ref, ab_ref = state[1], state[6], state[7]
    for h in range(N_HEADS):
        acc = ab_ref[h:h + 1, :] * acc_ref[h]
        for g, v_ref in enumerate(v_refs):
            acc = acc + v_ref[h] * p_ref[g, h:h + 1, :]
        acc_ref[h] = acc


def _decode_finish(o_ref, state):
    acc_ref, l_ref = state[1], state[3]
    for h in range(N_HEADS):
        acc_ref[h] = acc_ref[h] / l_ref[h:h + 1, :]
    ones = jnp.ones((8, LANES), BF16)
    o8 = _nt_dot3(ones, acc_ref[...].reshape(N_HEADS * HEAD_DIM, LANES))
    o_ref[...] = o8[0:1, :].astype(BF16)


def _decode_scratch(group):
    return [
        pltpu.VMEM((N_HEADS, HEAD_DIM, LANES), F32),
        pltpu.VMEM((N_HEADS, HEAD_DIM, LANES), F32),
        pltpu.VMEM((N_HEADS, 1), F32),
        pltpu.VMEM((N_HEADS, 1), F32),
        pltpu.VMEM((N_HEADS, 1), F32),
        pltpu.VMEM((group, N_HEADS, PAGE_SIZE), F32),
        pltpu.VMEM((group, N_HEADS, PAGE_SIZE), F32),
        pltpu.VMEM((N_HEADS, LANES), F32),
    ]


def _decode_specs(n_req, group, page_of):
    d = N_HEADS * HEAD_DIM
    small = lambda *a: (0, 0)
    kv_block = (None, N_HEADS, HEAD_DIM, PAGE_SIZE)
    kv_map = lambda g: (lambda *a: (page_of(g, *a), 0, 0, 0))
    lf_map = lambda g: (lambda *a: (page_of(g, *a), 0, 0))
    specs = [pl.BlockSpec((d, n_req), small)] * 3 + [pl.BlockSpec((N_HEADS, n_req), small)]
    specs += [pl.BlockSpec(kv_block, kv_map(g)) for g in range(group)]
    specs += [pl.BlockSpec(kv_block, kv_map(g)) for g in range(group)]
    specs += [pl.BlockSpec((None, N_HEADS, PAGE_SIZE), lf_map(g)) for g in range(group)]
    return specs


def _ffn_kernel(*refs, n_chunks, final_norm, dec):
    if dec is None:
        x_ref, z_ref, wmo_ref, g_ref, wg_ref, wu_ref, wo_ref, gf_ref, out_ref, acc_ref, h_ref = refs
    else:
        n_req, group, spr = dec
        x_ref, z_ref, wmo_ref, g_ref, wg_ref, wu_ref, wo_ref, gf_ref = refs[1:9]
        qT_ref, kTs_ref, vTs_ref, lfTs_ref = refs[9:13]
        k_refs = refs[13:13 + group]
        v_refs = refs[13 + group:13 + 2 * group]
        lf_refs = refs[13 + 2 * group:13 + 3 * group]
        out_ref, o_dec_ref, acc_ref, h_ref = refs[13 + 3 * group:17 + 3 * group]
        state = refs[17 + 3 * group:]
        step = pl.program_id(0) * n_chunks + pl.program_id(1)
    c = pl.program_id(1)

    @pl.when(c == 0)
    def _():
        y1 = x_ref[...] + _dot(z_ref[...], wmo_ref[...])
        acc_ref[...] = y1
        h_ref[...] = _rms(y1, g_ref[...]).astype(BF16)

    if dec is not None:
        active = step < n_req * spr

        @pl.when((step % spr == 0) & active)
        def _():
            _decode_init(step // spr, n_req, qT_ref, kTs_ref, vTs_ref, lfTs_ref, state)

    def chunk(with_pages):
        if with_pages:
            _decode_scores(k_refs, state)
            _decode_probs(lf_refs, state)
            _decode_values(v_refs, state)
        h = h_ref[...]
        gate = _dot(h, wg_ref[...])
        up = _dot(h, wu_ref[...])
        a = (gate * jax.nn.sigmoid(gate) * up).astype(BF16)
        acc_ref[...] += _dot(a, wo_ref[...])

    if dec is None:
        chunk(False)
    else:
        pl.when(active)(lambda: chunk(True))
        pl.when(jnp.logical_not(active))(lambda: chunk(False))

    @pl.when(c == n_chunks - 1)
    def _():
        y = acc_ref[...]
        out_ref[...] = _rms(y, gf_ref[...]) if final_norm else y

    if dec is not None:
        @pl.when((step % spr == spr - 1) & active)
        def _():
            _decode_finish(o_dec_ref, state)


def _mix_ffn(x, z, wmo, g, w_in, w_out, g_final, *, layer, tm, tc, final_norm, decode=None):
    m, d = x.shape
    d_ff = w_out.shape[1]
    n_chunks = d_ff // tc
    row = lambda i, c, *pt: (i, 0)
    const = lambda i, c, *pt: (0, 0)
    in_specs = [
        pl.BlockSpec((tm, d), row),
        pl.BlockSpec((tm, d), row),
        pl.BlockSpec((d, d), const),
        pl.BlockSpec((1, d), const),
        pl.BlockSpec((None, d, tc), lambda i, c, *pt: (layer, 0, c)),
        pl.BlockSpec((None, d, tc), lambda i, c, *pt: (layer, 0, n_chunks + c)),
        pl.BlockSpec((None, tc, d), lambda i, c, *pt: (layer, c, 0)),
        pl.BlockSpec((1, d), const),
    ]
    scratch = [pltpu.VMEM((tm, d), F32), pltpu.VMEM((tm, d), BF16)]
    operands = (x, z, wmo, g, w_in, w_in, w_out, g_final)
    grid = (m // tm, n_chunks)
    y_spec = pl.BlockSpec((tm, d), row)
    y_shape = jax.ShapeDtypeStruct((m, d), F32)
    if decode is None:
        return pl.pallas_call(
            functools.partial(_ffn_kernel, n_chunks=n_chunks, final_norm=final_norm, dec=None),
            grid=grid, in_specs=in_specs, out_specs=y_spec, out_shape=y_shape,
            scratch_shapes=scratch,
            compiler_params=_cparams(("arbitrary", "arbitrary")),
            name="mix_ffn",
        )(*operands)

    page_table, qT, kTs, vTs, lfTs, kT_cache, vT_cache, lfT_cache = decode
    n_req, n_pages = page_table.shape
    group = FFN_DEC_G
    spr = n_pages // group
    assert grid[0] * grid[1] >= n_req * spr, "not enough grid steps for these requests"

    def request_of(i, c):
        step = i * n_chunks + c
        valid = step < n_req * spr
        return jnp.where(valid, step // spr, n_req - 1), jnp.where(valid, step % spr, spr - 1)

    def page_of(g, i, c, pt):
        req, part = request_of(i, c)
        return pt[req, n_pages - 1 - (part * group + g)]

    grid_spec = pltpu.PrefetchScalarGridSpec(
        num_scalar_prefetch=1,
        grid=grid,
        in_specs=in_specs + _decode_specs(n_req, group, page_of),
        out_specs=[y_spec, pl.BlockSpec((None, 1, d), lambda i, c, pt: (request_of(i, c)[0], 0, 0))],
        scratch_shapes=scratch + _decode_scratch(group),
    )
    y, o_dec = pl.pallas_call(
        functools.partial(_ffn_kernel, n_chunks=n_chunks, final_norm=final_norm, dec=(n_req, group, spr)),
        grid_spec=grid_spec,
        out_shape=[y_shape, jax.ShapeDtypeStruct((n_req, 1, d), BF16)],
        compiler_params=_cparams(("arbitrary", "arbitrary")),
        name="mix_ffn_decode",
    )(page_table, *operands, qT, kTs, vTs, lfTs,
      *([kT_cache] * group), *([vT_cache] * group), *([lfT_cache] * group))
    return y, o_dec.reshape(n_req, d)


def _conv_prompt_kernel(x_ref, g_ref, wb_ref, wc_ref, wu_ref, ck_ref, prev_ref, z_ref, st_ref,
                        buf_ref, *, tm, blocks_per_seq):
    i = pl.program_id(0)
    h = _rms(x_ref[...], g_ref[...]).astype(BF16)
    gate = _dot(h, wb_ref[...])
    cu = _dot(h, wc_ref[...]) * _dot(h, wu_ref[...])

    @pl.when(i % blocks_per_seq == 0)
    def _():
        buf_ref[6:8, :] = prev_ref[...]

    buf_ref[8:8 + tm, :] = cu
    y = ck_ref[0:1, :] * buf_ref[6:6 + tm, :] + ck_ref[1:2, :] * buf_ref[7:7 + tm, :]
    y = y + ck_ref[2:3, :] * cu
    z_ref[...] = (gate * y).astype(BF16)
    tail = cu[tm - 2:tm, :]
    st_ref[...] = tail
    buf_ref[6:8, :] = tail


def _conv_prompt(x, g, w_in, ck, prev, *, batch, seq):
    m, d = x.shape
    tm = CONV_TM
    bps = seq // tm
    const = lambda i: (0, 0)
    return pl.pallas_call(
        functools.partial(_conv_prompt_kernel, tm=tm, blocks_per_seq=bps),
        grid=(m // tm,),
        in_specs=[
            pl.BlockSpec((tm, d), lambda i: (i, 0)),
            pl.BlockSpec((1, d), const),
            pl.BlockSpec((d, d), lambda i: (0, 0)),
            pl.BlockSpec((d, d), lambda i: (0, 1)),
            pl.BlockSpec((d, d), lambda i: (0, 2)),
            pl.BlockSpec((3, d), const),
            pl.BlockSpec((None, 2, d), lambda i: (i // bps, 0, 0)),
        ],
        out_specs=[
            pl.BlockSpec((tm, d), lambda i: (i, 0)),
            pl.BlockSpec((None, 2, d), lambda i: (i // bps, 0, 0)),
        ],
        out_shape=[
            jax.ShapeDtypeStruct((m, d), BF16),
            jax.ShapeDtypeStruct((batch, 2, d), F32),
        ],
        scratch_shapes=[pltpu.VMEM((tm + 8, d), F32)],
        compiler_params=_cparams(("arbitrary",)),
        name="conv_prompt",
    )(x, g, w_in, w_in, w_in, ck, prev)


def _conv_sample_kernel(x_ref, g_ref, wb_ref, wc_ref, wu_ref, ck_ref, p0_ref, p1_ref, z_ref, cu_ref):
    h = _rms(x_ref[...], g_ref[...]).astype(BF16)
    gate = _dot(h, wb_ref[...])
    cu = _dot(h, wc_ref[...]) * _dot(h, wu_ref[...])
    y = ck_ref[0:1, :] * p0_ref[...] + ck_ref[1:2, :] * p1_ref[...]
    y = y + ck_ref[2:3, :] * cu
    z_ref[...] = (gate * y).astype(BF16)
    cu_ref[...] = cu


def _conv_sample(x, g, w_in, ck, p0, p1):
    n, d = x.shape
    full = lambda shape: pl.BlockSpec(shape, lambda i: (0, 0))
    return pl.pallas_call(
        _conv_sample_kernel,
        grid=(1,),
        in_specs=[
            full((n, d)),
            full((1, d)),
            pl.BlockSpec((d, d), lambda i: (0, 0)),
            pl.BlockSpec((d, d), lambda i: (0, 1)),
            pl.BlockSpec((d, d), lambda i: (0, 2)),
            full((3, d)),
            full((n, d)),
            full((n, d)),
        ],
        out_specs=[full((n, d)), full((n, d))],
        out_shape=[jax.ShapeDtypeStruct((n, d), BF16), jax.ShapeDtypeStruct((n, d), F32)],
        compiler_params=_cparams(("arbitrary",)),
        name="conv_sample",
    )(x, g, w_in, w_in, w_in, ck, p0, p1)


def kernel(x_prompt, x_sample, cache_k, cache_v, cache_logf, state_conv, page_table,
           norm_mix, norm_ffn, w_ffn_in, w_ffn_out, norm_final,
           attn_w_qkvf, attn_b_f, attn_w_o, conv_w_in, conv_kernel, conv_w_out):
    batch, seq, d = x_prompt.shape
    n_req = x_sample.shape[0]
    assert x_sample.shape[1] == 1 and d == N_HEADS * HEAD_DIM
    assert norm_mix.shape[0] == 2 and cache_k.shape[2] == PAGE_SIZE

    xp = x_prompt.reshape(batch * seq, d)
    xs = x_sample.reshape(n_req, d)
    gfin = norm_final.reshape(1, d)

    wT = attn_w_qkvf[0].T.astype(BF16)
    bf = attn_b_f[0].reshape(N_HEADS, 1)
    g0 = norm_mix[0].reshape(1, d)

    qT, k, kT, vT, vTb, lfT = _proj_prompt(xp, g0, wT, bf, batch=batch, seq=seq)
    cumT, ka = _cumsum(lfT)
    n_layers, _, d_in2 = w_ffn_in.shape
    d_ff = w_ffn_out.shape[1]
    o_p, w_in, w_out = _attention(k, ka, qT, vTb, cumT, w_ffn_in.reshape(n_layers * d, d_in2),
                                  w_ffn_out.reshape(n_layers * d_ff, d), batch=batch, seq=seq)
    w_in = w_in.reshape(w_ffn_in.shape)
    w_out = w_out.reshape(w_ffn_out.shape)

    qTs, kTs, vTs, lfTs = _proj_sample(xs, g0, wT, bf)
    caches = (jnp.transpose(cache_k[0], (0, 2, 3, 1)),
              jnp.transpose(cache_v[0], (0, 2, 3, 1)),
              jnp.transpose(cache_logf[0], (0, 2, 1)))
    n_pages = page_table.shape[1]
    ffn_steps = (batch * seq // FFN_TM) * (w_ffn_out.shape[1] // FFN_TC)
    capacity = ffn_steps // (n_pages // FFN_DEC_G)
    n_ride = (n_req + 1) // 2
    assert n_ride <= capacity

    def requests(lo, hi):
        return (page_table[lo:hi], qTs[:, lo:hi], kTs[:, lo:hi], vTs[:, lo:hi], lfTs[:, lo:hi], *caches)

    wo = attn_w_o[0].astype(BF16)
    gf0 = norm_ffn[0].reshape(1, d)
    yp, o_s0 = _mix_ffn(xp, o_p, wo, gf0, w_in, w_out, gfin, layer=0, tm=FFN_TM, tc=FFN_TC,
                        final_norm=False, decode=requests(0, n_ride))

    wci = conv_w_in[0].astype(BF16)
    g1 = norm_mix[1].reshape(1, d)
    ck = conv_kernel[0]
    wco = conv_w_out[0].astype(BF16)
    gf1 = norm_ffn[1].reshape(1, d)
    zp, conv_p = _conv_prompt(yp, g1, wci, ck, jnp.zeros((batch, 2, d), F32), batch=batch, seq=seq)
    yp, o_s1 = _mix_ffn(yp, zp, wco, gf1, w_in, w_out, gfin, layer=1, tm=FFN_TM, tc=FFN_TC,
                        final_norm=True, decode=requests(n_ride, n_req))

    o_s = jnp.concatenate([o_s0, o_s1], axis=0)
    tc_s = w_ffn_out.shape[1] // 2
    ys = _mix_ffn(xs, o_s, wo, gf0, w_in, w_out, gfin, layer=0, tm=n_req, tc=tc_s, final_norm=False)
    zs, cu_s = _conv_sample(ys, g1, wci, ck, state_conv[0, :, 0], state_conv[0, :, 1])
    ys = _mix_ffn(ys, zs, wco, gf1, w_in, w_out, gfin, layer=1, tm=n_req, tc=tc_s, final_norm=True)

    to_heads = lambda t: jnp.transpose(t.reshape(batch, N_HEADS, HEAD_DIM, seq), (0, 3, 1, 2))[None]
    to_heads_s = lambda t: t.T.reshape(1, n_req, 1, N_HEADS, HEAD_DIM)
    return (
        yp.reshape(batch, seq, d),
        ys.reshape(n_req, 1, d),
        to_heads(kT),
        to_heads(vT),
        jnp.transpose(lfT, (0, 2, 1))[None],
        to_heads_s(kTs),
        to_heads_s(vTs),
        lfTs.T.reshape(1, n_req, 1, N_HEADS),
        conv_p[None],
        jnp.stack([state_conv[0, :, 1], cu_s], axis=1)[None],
    )
```
